```python
import jax, jax.numpy as jnp
from jax import lax
import numpy as np

D_MODEL = 1024
BATCH = 4
SEQ = 8192
DEPTH = 2

CHUNK = 64
Q_BLOCK = 128
FOX_HEADS = 8
FOX_HEAD_DIM = 64
FOX_WIDTH = FOX_HEADS * FOX_HEAD_DIM
CONV_WIDTH = 512
CONV_K = 3
EVEN_IN = 3 * FOX_WIDTH + FOX_HEADS + 3 * CONV_WIDTH
EVEN_MIX = FOX_WIDTH + CONV_WIDTH
GMLP_BLOCK = 128
GMLP_GROUPS = 8
GMLP_WIDTH = D_MODEL
GMLP_GROUP_DIM = GMLP_WIDTH // GMLP_GROUPS
FFN_HIDDEN = -(-8 * D_MODEL // (3 * 256)) * 256
ALPHA = (2.0 * DEPTH) ** 0.25
BETA = (8.0 * DEPTH) ** -0.25
N_EVEN = (DEPTH + 1) // 2
N_ODD = DEPTH // 2
LN_EPS = 1e-5

kernel_name = "fox_shortconv_gmlp_deepnorm_trunk"


def layer_norm(x, g, b):
    xf = x.astype(jnp.float32)
    mu = jnp.mean(xf, axis=-1, keepdims=True)
    var = jnp.mean(jnp.square(xf - mu), axis=-1, keepdims=True)
    return ((xf - mu) * lax.rsqrt(var + LN_EPS) * g + b).astype(x.dtype)


def forgetting_attention(q, k, v, log_f):
    bsz, s_len, h, dh = q.shape
    nb = s_len // Q_BLOCK
    c = jnp.cumsum(log_f, axis=1).transpose(0, 2, 1)
    kh = k.transpose(0, 2, 1, 3)
    vh = v.transpose(0, 2, 1, 3)
    qb = q.reshape(bsz, nb, Q_BLOCK, h, dh).transpose(1, 0, 3, 2, 4)
    cb = c.reshape(bsz, h, nb, Q_BLOCK).transpose(2, 0, 1, 3)
    pos = jnp.arange(s_len)
    posb = pos.reshape(nb, Q_BLOCK)
    scale = dh ** -0.5

    def block(args):
        q_blk, c_blk, p_blk = args
        s = jnp.einsum('bhqd,bhkd->bhqk', q_blk, kh,
                       preferred_element_type=jnp.float32) * scale
        s = s + c_blk[..., :, None] - c[..., None, :]
        s = jnp.where(p_blk[:, None] >= pos[None, :], s, -jnp.inf)
        p = jax.nn.softmax(s, axis=-1)
        return jnp.einsum('bhqk,bhkd->bhqd', p.astype(vh.dtype), vh)

    o = lax.map(block, (qb, cb, posb))
    return o.transpose(1, 0, 3, 2, 4).reshape(bsz, s_len, h * dh)


def short_conv_mixer(h, b_gate, c_gate, conv_w):
    s_len = h.shape[1]
    z = c_gate * h
    zp = jnp.pad(z, ((0, 0), (CONV_K - 1, 0), (0, 0)))
    y = conv_w[0] * zp[:, 0:s_len]
    for i in range(1, CONV_K):
        y = y + conv_w[i] * zp[:, i:i + s_len]
    return b_gate * y


def fox_conv_mixer(x, w_in, b_f, conv_w, w_out):
    bsz, s_len, _ = x.shape
    proj = x @ w_in
    cuts = np.cumsum([FOX_WIDTH, FOX_WIDTH, FOX_WIDTH, FOX_HEADS, CONV_WIDTH, CONV_WIDTH]).tolist()
    q, k, v, f_logit, b_gate, c_gate, h = jnp.split(proj, cuts, axis=-1)
    log_f = jax.nn.log_sigmoid((f_logit + b_f).astype(jnp.float32))
    heads = (bsz, s_len, FOX_HEADS, FOX_HEAD_DIM)
    attn = forgetting_attention(q.reshape(heads), k.reshape(heads), v.reshape(heads), log_f)
    conv = short_conv_mixer(h, b_gate, c_gate, conv_w)
    return jnp.concatenate([attn.astype(x.dtype), conv], axis=-1) @ w_out


def gmlp_mixer(x, w_in, v_ln_g, v_ln_b, w_s, b_s, w_out):
    bsz, s_len, _ = x.shape
    uv = jax.nn.gelu(x @ w_in, approximate=False)
    u, v = jnp.split(uv, 2, axis=-1)
    v = layer_norm(v, v_ln_g, v_ln_b)
    nc = s_len // GMLP_BLOCK
    vb = v.reshape(bsz, nc, GMLP_BLOCK, GMLP_GROUPS, GMLP_GROUP_DIM)
    chunk_id = jnp.arange(GMLP_BLOCK) // CHUNK
    mask = chunk_id[None, :] <= chunk_id[:, None]
    w = jnp.where(mask[None], w_s, jnp.zeros((), w_s.dtype))
    s = jnp.einsum('gij,bcjgd->bcigd', w, vb) + b_s.T[None, None, :, :, None]
    return (u * s.reshape(bsz, s_len, GMLP_WIDTH)) @ w_out


def swiglu(x, w_in, w_out):
    gate, up = jnp.split(x @ w_in, 2, axis=-1)
    return (jax.nn.silu(gate) * up) @ w_out


def setup_inputs(seed: int = 0) -> dict:
    key = jax.random.key(seed)
    ks = jax.random.split(key, 20)
    nrm = jax.random.normal
    f32 = jnp.float32
    return {
        "x": nrm(ks[0], (BATCH, SEQ, D_MODEL), f32),
        "even_w_in": nrm(ks[1], (N_EVEN, D_MODEL, EVEN_IN), f32) * D_MODEL ** -0.5,
        "even_b_f": jax.random.uniform(ks[2], (N_EVEN, FOX_HEADS), f32, 1.0, 5.0),
        "even_conv_w": nrm(ks[3], (N_EVEN, CONV_K, CONV_WIDTH), f32) * CONV_K ** -0.5,
        "even_w_out": nrm(ks[4], (N_EVEN, EVEN_MIX, D_MODEL), f32) * (EVEN_MIX ** -0.5 * BETA),
        "odd_w_in": nrm(ks[5], (N_ODD, D_MODEL, 2 * GMLP_WIDTH), f32) * D_MODEL ** -0.5,
        "odd_v_ln_g": 1.0 + 0.1 * nrm(ks[6], (N_ODD, GMLP_WIDTH), f32),
        "odd_v_ln_b": 0.1 * nrm(ks[7], (N_ODD, GMLP_WIDTH), f32),
        "odd_w_s": nrm(ks[8], (N_ODD, GMLP_GROUPS, GMLP_BLOCK, GMLP_BLOCK), f32) * GMLP_BLOCK ** -0.5,
        "odd_b_s": 1.0 + 0.1 * nrm(ks[9], (N_ODD, GMLP_GROUPS, GMLP_BLOCK), f32),
        "odd_w_out": nrm(ks[10], (N_ODD, GMLP_WIDTH, D_MODEL), f32) * (GMLP_WIDTH ** -0.5 * BETA),
        "mix_ln_g": 1.0 + 0.1 * nrm(ks[11], (DEPTH, D_MODEL), f32),
        "mix_ln_b": 0.1 * nrm(ks[12], (DEPTH, D_MODEL), f32),
        "ffn_w_in": nrm(ks[13], (DEPTH, D_MODEL, 2 * FFN_HIDDEN), f32) * D_MODEL ** -0.5,
        "ffn_w_out": nrm(ks[14], (DEPTH, FFN_HIDDEN, D_MODEL), f32) * (FFN_HIDDEN ** -0.5 * BETA),
        "ffn_ln_g": 1.0 + 0.1 * nrm(ks[15], (DEPTH, D_MODEL), f32),
        "ffn_ln_b": 0.1 * nrm(ks[16], (DEPTH, D_MODEL), f32),
    }


def reference(x, even_w_in, even_b_f, even_conv_w, even_w_out, odd_w_in, odd_v_ln_g,
              odd_v_ln_b, odd_w_s, odd_b_s, odd_w_out, mix_ln_g, mix_ln_b, ffn_w_in,
              ffn_w_out, ffn_ln_g, ffn_ln_b):
    for layer in range(DEPTH):
        i = layer // 2
        if layer % 2 == 0:
            m = fox_conv_mixer(x, even_w_in[i], even_b_f[i], even_conv_w[i], even_w_out[i])
        else:
            m = gmlp_mixer(x, odd_w_in[i], odd_v_ln_g[i], odd_v_ln_b[i], odd_w_s[i],
                           odd_b_s[i], odd_w_out[i])
        x = layer_norm(ALPHA * x + m, mix_ln_g[layer], mix_ln_b[layer])
        x = layer_norm(ALPHA * x + swiglu(x, ffn_w_in[layer], ffn_w_out[layer]),
                       ffn_ln_g[layer], ffn_ln_b[layer])
    return x
```

```python
import functools

import jax
import jax.numpy as jnp
from jax import lax
from jax.experimental import pallas as pl
from jax.experimental.pallas import tpu as pltpu

D_MODEL = 1024
HEADS = 8
HEAD_DIM = 64
FOX_WIDTH = HEADS * HEAD_DIM
CONV_WIDTH = 512
CONV_K = 3
GMLP_BLOCK = 128
GMLP_GROUPS = 8
CHUNK = 64
FFN_HIDDEN = 2816
DEPTH = 2
ALPHA = (2.0 * DEPTH) ** 0.25
LN_EPS = 1e-5

LANES = 128
F_PAD = LANES
IN0_COLS = 3 * FOX_WIDTH + 3 * CONV_WIDTH + F_PAD
VMEM_LIMIT = 56 * 1024 * 1024

ROWS_IN0 = 512
ROWS_MLP = 512
ATT_TQ = 256
ATT_TK = 512
FFN_CHUNK = 256
GMLP_IN_CHUNK = 512

BF16 = jnp.bfloat16
F32 = jnp.float32


def _dot(a, b):
    return jnp.dot(a, b, preferred_element_type=F32)


def _layer_norm(y, g, b):
    mu = jnp.mean(y, axis=-1, keepdims=True)
    yc = y - mu
    var = jnp.mean(yc * yc, axis=-1, keepdims=True)
    return yc * lax.rsqrt(var + LN_EPS) * g + b


def _inproj0_kernel(x_ref, w_ref, bf_ref, cw_ref, tri_ref,
                    q_ref, k_ref, v_ref, conv_ref, ccol_ref, crow_ref,
                    xb_ref, zbuf_ref, carry_ref):
    tm = x_ref.shape[0]

    @pl.when(pl.program_id(1) == 0)
    def _():
        zbuf_ref[0:8, :] = jnp.zeros((8, CONV_WIDTH), F32)
        carry_ref[...] = jnp.zeros_like(carry_ref)

    xb_ref[...] = x_ref[...].astype(BF16)
    xb = xb_ref[...]
    w = FOX_WIDTH
    q_ref[...] = (_dot(xb, w_ref[:, 0:w]) * (HEAD_DIM ** -0.5)).astype(BF16)
    k_ref[...] = _dot(xb, w_ref[:, w:2 * w]).astype(BF16)
    v_ref[...] = _dot(xb, w_ref[:, 2 * w:3 * w]).astype(BF16)

    o = 3 * w
    cg = _dot(xb, w_ref[:, o + CONV_WIDTH:o + 2 * CONV_WIDTH])
    hh = _dot(xb, w_ref[:, o + 2 * CONV_WIDTH:o + 3 * CONV_WIDTH])
    z = cg * hh
    zbuf_ref[8:8 + tm, :] = z
    y = (cw_ref[0:1, :] * zbuf_ref[6:6 + tm, :]
         + cw_ref[1:2, :] * zbuf_ref[7:7 + tm, :]
         + cw_ref[2:3, :] * z)
    bg = _dot(xb, w_ref[:, o:o + CONV_WIDTH])
    conv_ref[...] = (bg * y).astype(BF16)
    zbuf_ref[0:8, :] = zbuf_ref[tm:tm + 8, :]

    fl = _dot(xb, w_ref[:, o + 3 * CONV_WIDTH:o + 3 * CONV_WIDTH + F_PAD]) + bf_ref[...]
    logf = jnp.minimum(fl, 0.0) - jnp.log1p(jnp.exp(-jnp.abs(fl)))
    lane = lax.broadcasted_iota(jnp.int32, logf.shape, 1)
    logf = jnp.where(lane < HEADS, logf, 0.0)
    hi = logf.astype(BF16)
    r1 = logf - hi.astype(F32)
    mid = r1.astype(BF16)
    lo = (r1 - mid.astype(F32)).astype(BF16)
    tri = tri_ref[...]
    c = (_dot(tri, hi) + _dot(tri, mid)) + _dot(tri, lo) + carry_ref[...]
    ccol_ref[...] = c
    carry_ref[...] = c[tm - 1:tm, :]
    crow_ref[...] = c.T[0:HEADS, :]


def _inproj0(x, w_all, bf_pad, conv_w, tri):
    bsz, s_len, _ = x.shape
    tm = ROWS_IN0
    grid = (bsz, s_len // tm)
    row = lambda b, i: (b, i, 0)
    const2 = lambda b, i: (0, 0)
    out_bf = jax.ShapeDtypeStruct((bsz, s_len, FOX_WIDTH), BF16)
    return pl.pallas_call(
        _inproj0_kernel,
        grid=grid,
        in_specs=[
            pl.BlockSpec((None, tm, D_MODEL), row),
            pl.BlockSpec((D_MODEL, IN0_COLS), const2),
            pl.BlockSpec((1, F_PAD), const2),
            pl.BlockSpec((CONV_K, CONV_WIDTH), const2),
            pl.BlockSpec((tm, tm), const2),
        ],
        out_specs=[
            pl.BlockSpec((None, tm, FOX_WIDTH), row),
            pl.BlockSpec((None, tm, FOX_WIDTH), row),
            pl.BlockSpec((None, tm, FOX_WIDTH), row),
            pl.BlockSpec((None, tm, CONV_WIDTH), row),
            pl.BlockSpec((None, tm, LANES), row),
            pl.BlockSpec((None, HEADS, tm), lambda b, i: (b, 0, i)),
        ],
        out_shape=[
            out_bf, out_bf, out_bf,
            jax.ShapeDtypeStruct((bsz, s_len, CONV_WIDTH), BF16),
            jax.ShapeDtypeStruct((bsz, s_len, LANES), F32),
            jax.ShapeDtypeStruct((bsz, HEADS, s_len), F32),
        ],
        scratch_shapes=[
            pltpu.VMEM((tm, D_MODEL), BF16),
            pltpu.VMEM((tm + 8, CONV_WIDTH), F32),
            pltpu.VMEM((1, LANES), F32),
        ],
        compiler_params=pltpu.CompilerParams(
            dimension_semantics=("arbitrary", "arbitrary"),
            vmem_limit_bytes=VMEM_LIMIT),
        name="inproj0",
    )(x, w_all, bf_pad, conv_w, tri)


def _attn_kernel(q_ref, k_ref, v_ref, ccol_ref, crow_ref, o_ref, m_ref, l_ref, acc_ref):
    tq = q_ref.shape[0]
    tk = ATT_TK
    pair = pl.program_id(1)
    qi = pl.program_id(2)
    n_lane_tiles = tk // LANES

    q2 = q_ref[...]
    lane = lax.broadcasted_iota(jnp.int32, (tq, LANES), 1)
    ccol = ccol_ref[...]
    qh, ct = [], []
    for hh in range(2):
        in_half = (lane >= hh * HEAD_DIM) & (lane < (hh + 1) * HEAD_DIM)
        qh.append(jnp.where(in_half, q2, jnp.zeros_like(q2)))
        head = 2 * pair + hh
        ct.append(jnp.broadcast_to(
            jnp.sum(jnp.where(lane == head, ccol, 0.0), axis=1, keepdims=True), (tq, LANES)))

    m_ref[...] = jnp.full(m_ref.shape, -jnp.inf, F32)
    l_ref[...] = jnp.zeros(l_ref.shape, F32)
    acc_ref[...] = jnp.zeros(acc_ref.shape, F32)

    def tile_update(j, masked):
        ks = pl.multiple_of(j * tk, tk)
        kt = k_ref[pl.ds(ks, tk), :]
        vt = v_ref[pl.ds(ks, tk), :]
        if masked:
            row_g = qi * tq + lax.broadcasted_iota(jnp.int32, (tq, LANES), 0)
            col_g = ks + lax.broadcasted_iota(jnp.int32, (tq, LANES), 1)
        for hh in range(2):
            s = lax.dot_general(qh[hh], kt, (((1,), (1,)), ((), ())),
                                preferred_element_type=F32)
            cs = crow_ref[hh:hh + 1, pl.ds(ks, tk)]
            sc = []
            for c in range(n_lane_tiles):
                sl = slice(c * LANES, (c + 1) * LANES)
                t = (s[:, sl] + ct[hh]) - cs[:, sl]
                if masked:
                    t = jnp.where(row_g >= col_g + c * LANES, t, -jnp.inf)
                sc.append(t)
            mpart = sc[0]
            for c in range(1, n_lane_tiles):
                mpart = jnp.maximum(mpart, sc[c])
            m_prev = m_ref[hh]
            m_new = jnp.maximum(m_prev, jnp.max(mpart, axis=1, keepdims=True))
            alpha = jnp.exp(m_prev - m_new)
            ps = [jnp.exp(t - m_new) for t in sc]
            lsum = ps[0]
            for c in range(1, n_lane_tiles):
                lsum = lsum + ps[c]
            p = jnp.concatenate([t.astype(BF16) for t in ps], axis=1)
            m_ref[hh] = m_new
            l_ref[hh] = alpha * l_ref[hh] + lsum
            acc_ref[hh] = alpha * acc_ref[hh] + _dot(p, vt)

    n_full = (qi * tq) // tk

    def body(j, carry):
        tile_update(j, False)
        return carry

    lax.fori_loop(0, n_full, body, 0)
    tile_update(n_full, True)

    outs = []
    for hh in range(2):
        l_row = jnp.sum(l_ref[hh], axis=1, keepdims=True)
        outs.append(acc_ref[hh] / l_row)
    o_ref[...] = jnp.where(lane < HEAD_DIM, outs[0], outs[1]).astype(o_ref.dtype)


def _attention(q, k, v, ccol, crow_pairs):
    bsz, s_len, _ = q.shape
    tq = ATT_TQ
    n_pairs = HEADS // 2
    grid = (bsz, n_pairs, s_len // tq)
    return pl.pallas_call(
        _attn_kernel,
        grid=grid,
        in_specs=[
            pl.BlockSpec((None, tq, LANES), lambda b, p, i: (b, i, p)),
            pl.BlockSpec((None, s_len, LANES), lambda b, p, i: (b, 0, p)),
            pl.BlockSpec((None, s_len, LANES), lambda b, p, i: (b, 0, p)),
            pl.BlockSpec((None, tq, LANES), lambda b, p, i: (b, i, 0)),
            pl.BlockSpec((None, None, 2, s_len), lambda b, p, i: (b, p, 0, 0)),
        ],
        out_specs=pl.BlockSpec((None, tq, LANES), lambda b, p, i: (b, i, p)),
        out_shape=jax.ShapeDtypeStruct((bsz, s_len, FOX_WIDTH), BF16),
        scratch_shapes=[
            pltpu.VMEM((2, tq, LANES), F32),
            pltpu.VMEM((2, tq, LANES), F32),
            pltpu.VMEM((2, tq, LANES), F32),
        ],
        compiler_params=pltpu.CompilerParams(
            dimension_semantics=("arbitrary", "arbitrary", "arbitrary"),
            vmem_limit_bytes=VMEM_LIMIT),
        name="fox_attention",
    )(q, k, v, ccol, crow_pairs)


def _outproj0_kernel(attn_ref, conv_ref, x_ref, w_ref, g_ref, b_ref, o_ref):
    m = (_dot(attn_ref[...], w_ref[0:FOX_WIDTH, :])
         + _dot(conv_ref[...], w_ref[FOX_WIDTH:FOX_WIDTH + CONV_WIDTH, :]))
    o_ref[...] = _layer_norm(ALPHA * x_ref[...] + m, g_ref[...], b_ref[...])


def _outproj0(attn, conv, x2, w_out, g, b):
    n = x2.shape[0]
    tm = ROWS_MLP
    row = lambda i: (i, 0)
    const = lambda i: (0, 0)
    return pl.pallas_call(
        _outproj0_kernel,
        grid=(n // tm,),
        in_specs=[
            pl.BlockSpec((tm, FOX_WIDTH), row),
            pl.BlockSpec((tm, CONV_WIDTH), row),
            pl.BlockSpec((tm, D_MODEL), row),
            pl.BlockSpec((FOX_WIDTH + CONV_WIDTH, D_MODEL), const),
            pl.BlockSpec((1, D_MODEL), const),
            pl.BlockSpec((1, D_MODEL), const),
        ],
        out_specs=pl.BlockSpec((tm, D_MODEL), row),
        out_shape=jax.ShapeDtypeStruct((n, D_MODEL), F32),
        compiler_params=pltpu.CompilerParams(
            dimension_semantics=("arbitrary",), vmem_limit_bytes=VMEM_LIMIT),
        name="outproj0_ln",
    )(attn, conv, x2, w_out, g, b)


def _ffn_kernel(x_ref, win_ref, wout_ref, g_ref, b_ref, o_ref, xb_ref, a_ref):
    xb_ref[...] = x_ref[...].astype(BF16)
    xb = xb_ref[...]
    for c in range(FFN_HIDDEN // FFN_CHUNK):
        lo = c * FFN_CHUNK
        gate = _dot(xb, win_ref[:, lo:lo + FFN_CHUNK])
        up = _dot(xb, win_ref[:, FFN_HIDDEN + lo:FFN_HIDDEN + lo + FFN_CHUNK])
        a_ref[:, lo:lo + FFN_CHUNK] = (gate * (1.0 / (1.0 + jnp.exp(-gate))) * up).astype(BF16)
    m = _dot(a_ref[...], wout_ref[...])
    o_ref[...] = _layer_norm(ALPHA * x_ref[...] + m, g_ref[...], b_ref[...])


def _ffn(x2, w_in, w_out, g, b):
    n = x2.shape[0]
    tm = ROWS_MLP
    row = lambda i: (i, 0)
    const = lambda i: (0, 0)
    return pl.pallas_call(
        _ffn_kernel,
        grid=(n // tm,),
        in_specs=[
            pl.BlockSpec((tm, D_MODEL), row),
            pl.BlockSpec((D_MODEL, 2 * FFN_HIDDEN), const),
            pl.BlockSpec((FFN_HIDDEN, D_MODEL), const),
            pl.BlockSpec((1, D_MODEL), const),
            pl.BlockSpec((1, D_MODEL), const),
        ],
        out_specs=pl.BlockSpec((tm, D_MODEL), row),
        out_shape=jax.ShapeDtypeStruct((n, D_MODEL), F32),
        scratch_shapes=[
            pltpu.VMEM((tm, D_MODEL), BF16),
            pltpu.VMEM((tm, FFN_HIDDEN), BF16),
        ],
        compiler_params=pltpu.CompilerParams(
            dimension_semantics=("arbitrary",), vmem_limit_bytes=VMEM_LIMIT),
        name="swiglu_ln",
    )(x2, w_in, w_out, g, b)


def _gmlp_kernel(x_ref, win_ref, vg_ref, vb_ref, ws_ref, bst_ref, wout_ref, g_ref, b_ref,
                 o_ref, xb_ref, u_ref, v_ref, vn_ref, t_ref):
    tm = x_ref.shape[0]
    n_blk = tm // GMLP_BLOCK
    xb_ref[...] = x_ref[...].astype(BF16)
    xb = xb_ref[...]
    for c in range(2 * D_MODEL // GMLP_IN_CHUNK):
        lo = c * GMLP_IN_CHUNK
        t = _dot(xb, win_ref[:, lo:lo + GMLP_IN_CHUNK])
        t = 0.5 * t * (1.0 + lax.erf(t * (2.0 ** -0.5)))
        if lo < D_MODEL:
            u_ref[:, lo:lo + GMLP_IN_CHUNK] = t
        else:
            v_ref[:, lo - D_MODEL:lo - D_MODEL + GMLP_IN_CHUNK] = t
    vn_ref[...] = _layer_norm(v_ref[...], vg_ref[...], vb_ref[...]).astype(BF16)

    pos_i = lax.broadcasted_iota(jnp.int32, (GMLP_BLOCK, GMLP_BLOCK), 0) // CHUNK
    pos_j = lax.broadcasted_iota(jnp.int32, (GMLP_BLOCK, GMLP_BLOCK), 1) // CHUNK
    causal = pos_j <= pos_i
    for g in range(GMLP_GROUPS):
        gl = slice(g * GMLP_BLOCK, (g + 1) * GMLP_BLOCK)
        w = jnp.where(causal, ws_ref[g], 0.0).astype(BF16)
        rhs = jnp.concatenate(
            [vn_ref[r * GMLP_BLOCK:(r + 1) * GMLP_BLOCK, gl] for r in range(n_blk)], axis=1)
        sg = _dot(w, rhs) + bst_ref[:, g:g + 1]
        for r in range(n_blk):
            rows = slice(r * GMLP_BLOCK, (r + 1) * GMLP_BLOCK)
            t_ref[rows, gl] = (u_ref[rows, gl]
                               * sg[:, r * GMLP_BLOCK:(r + 1) * GMLP_BLOCK]).astype(BF16)
    m = _dot(t_ref[...], wout_ref[...])
    o_ref[...] = _layer_norm(ALPHA * x_ref[...] + m, g_ref[...], b_ref[...])


def _gmlp(x2, w_in, v_g, v_b, w_s, b_s_t, w_out, g, b):
    n = x2.shape[0]
    tm = ROWS_MLP
    row = lambda i: (i, 0)
    const = lambda i: (0, 0)
    return pl.pallas_call(
        _gmlp_kernel,
        grid=(n // tm,),
        in_specs=[
            pl.BlockSpec((tm, D_MODEL), row),
            pl.BlockSpec((D_MODEL, 2 * D_MODEL), const),
            pl.BlockSpec((1, D_MODEL), const),
            pl.BlockSpec((1, D_MODEL), const),
            pl.BlockSpec((GMLP_GROUPS, GMLP_BLOCK, GMLP_BLOCK), lambda i: (0, 0, 0)),
            pl.BlockSpec((GMLP_BLOCK, GMLP_GROUPS), const),
            pl.BlockSpec((D_MODEL, D_MODEL), const),
            pl.BlockSpec((1, D_MODEL), const),
            pl.BlockSpec((1, D_MODEL), const),
        ],
        out_specs=pl.BlockSpec((tm, D_MODEL), row),
        out_shape=jax.ShapeDtypeStruct((n, D_MODEL), F32),
        scratch_shapes=[
            pltpu.VMEM((tm, D_MODEL), BF16),
            pltpu.VMEM((tm, D_MODEL), F32),
            pltpu.VMEM((tm, D_MODEL), F32),
            pltpu.VMEM((tm, D_MODEL), BF16),
            pltpu.VMEM((tm, D_MODEL), BF16),
        ],
        compiler_params=pltpu.CompilerParams(
            dimension_semantics=("arbitrary",), vmem_limit_bytes=VMEM_LIMIT),
        name="gmlp_ln",
    )(x2, w_in, v_g, v_b, w_s, b_s_t, w_out, g, b)


def kernel(x, even_w_in, even_b_f, even_conv_w, even_w_out, odd_w_in, odd_v_ln_g, odd_v_ln_b,
           odd_w_s, odd_b_s, odd_w_out, mix_ln_g, mix_ln_b, ffn_w_in, ffn_w_out, ffn_ln_g,
           ffn_ln_b):
    bsz, s_len, _ = x.shape
    n = bsz * s_len
    row = lambda a: a.reshape(1, -1)

    w0 = even_w_in[0]
    qkv_end = 3 * FOX_WIDTH
    w0 = jnp.concatenate(
        [w0[:, :qkv_end], w0[:, qkv_end + HEADS:], w0[:, qkv_end:qkv_end + HEADS],
         jnp.zeros((D_MODEL, F_PAD - HEADS), F32)], axis=1).astype(BF16)
    bf_pad = jnp.pad(even_b_f[0], (0, F_PAD - HEADS)).reshape(1, F_PAD)
    tri = jnp.tril(jnp.ones((ROWS_IN0, ROWS_IN0), BF16))

    q, k, v, conv, ccol, crow = _inproj0(x, w0, bf_pad, even_conv_w[0], tri)
    attn = _attention(q, k, v, ccol, crow.reshape(bsz, HEADS // 2, 2, s_len))
    x2 = x.reshape(n, D_MODEL)
    x2 = _outproj0(attn.reshape(n, FOX_WIDTH), conv.reshape(n, CONV_WIDTH), x2,
                   even_w_out[0].astype(BF16), row(mix_ln_g[0]), row(mix_ln_b[0]))
    x2 = _ffn(x2, ffn_w_in[0].astype(BF16), ffn_w_out[0].astype(BF16),
              row(ffn_ln_g[0]), row(ffn_ln_b[0]))
    x2 = _gmlp(x2, odd_w_in[0].astype(BF16), row(odd_v_ln_g[0]), row(odd_v_ln_b[0]),
               odd_w_s[0], odd_b_s[0].T, odd_w_out[0].astype(BF16),
               row(mix_ln_g[1]), row(mix_ln_b[1]))
    x2 = _ffn(x2, ffn_w_in[1].astype(BF16), ffn_w_out[1].astype(BF16),
              row(ffn_ln_g[1]), row(ffn_ln_b[1]))
    return x2.reshape(bsz, s_len, D_MODEL)
```

```python
import jax
import jax.numpy as jnp
import numpy as np
from jax import lax
from jax.experimental import pallas as pl
from jax.experimental.pallas import tpu as pltpu

D_MODEL = 1024
HEADS = 8
HEAD_DIM = 64
FOX_WIDTH = HEADS * HEAD_DIM
CONV_WIDTH = 512
CONV_K = 3
GMLP_BLOCK = 128
GMLP_GROUPS = 8
CHUNK = 64
FFN_HIDDEN = 2816
DEPTH = 2
ALPHA = (2.0 * DEPTH) ** 0.25
LN_EPS = 1e-5

LANES = 128
F_PAD = LANES
IN0_COLS = 2 * FOX_WIDTH + 3 * CONV_WIDTH + F_PAD
BIAS_LANES = 6
LOG2E = 1.4426950408889634
Q_SCALE = HEAD_DIM ** -0.5 * LOG2E
VMEM_LIMIT = 56 * 1024 * 1024

ROWS_IN0 = 512
ROWS_MLP = 512
ATT_TQ = 256
ATT_TK = 512
FFN_CHUNK = 256
GMLP_IN_CHUNK = 512

BF16 = jnp.bfloat16
F32 = jnp.float32


def _dot(a, b):
    return jnp.dot(a, b, preferred_element_type=F32)


def _layer_norm(y, g, b):
    mu = jnp.mean(y, axis=-1, keepdims=True)
    yc = y - mu
    var = jnp.mean(yc * yc, axis=-1, keepdims=True)
    return yc * lax.rsqrt(var + LN_EPS) * g + b


def _split3(a):
    hi = a.astype(BF16)
    r1 = a - hi.astype(F32)
    mid = r1.astype(BF16)
    lo = (r1 - mid.astype(F32)).astype(BF16)
    return hi, mid, lo


def _inproj0_kernel(x_ref, w_ref, wvt_ref, bf_ref, cw_ref, tri_ref, place_ref, ones_ref,
                    q_ref, k_ref, vt_ref, conv_ref, qa_ref, ka_ref,
                    xb_ref, zbuf_ref, carry_ref):
    tm = x_ref.shape[0]

    @pl.when(pl.program_id(1) == 0)
    def _():
        zbuf_ref[0:8, :] = jnp.zeros((8, CONV_WIDTH), F32)
        carry_ref[...] = jnp.zeros_like(carry_ref)

    xb_ref[...] = x_ref[...].astype(BF16)
    xb = xb_ref[...]
    w = FOX_WIDTH
    q_ref[...] = (_dot(xb, w_ref[:, 0:w]) * Q_SCALE).astype(BF16)
    k_ref[...] = _dot(xb, w_ref[:, w:2 * w]).astype(BF16)
    vt = lax.dot_general(wvt_ref[...], xb, (((1,), (1,)), ((), ())),
                         preferred_element_type=F32)
    vt_ref[...] = vt.astype(BF16).reshape(HEADS // 2, LANES, tm)

    o = 2 * w
    cg = _dot(xb, w_ref[:, o + CONV_WIDTH:o + 2 * CONV_WIDTH])
    hh = _dot(xb, w_ref[:, o + 2 * CONV_WIDTH:o + 3 * CONV_WIDTH])
    z = cg * hh
    zbuf_ref[8:8 + tm, :] = z
    y = (cw_ref[0:1, :] * zbuf_ref[6:6 + tm, :]
         + cw_ref[1:2, :] * zbuf_ref[7:7 + tm, :]
         + cw_ref[2:3, :] * z)
    bg = _dot(xb, w_ref[:, o:o + CONV_WIDTH])
    conv_ref[...] = (bg * y).astype(BF16)
    zbuf_ref[0:8, :] = zbuf_ref[tm:tm + 8, :]

    fl = _dot(xb, w_ref[:, o + 3 * CONV_WIDTH:o + 3 * CONV_WIDTH + F_PAD]) + bf_ref[...]
    logf = jnp.minimum(fl, 0.0) - jnp.log1p(jnp.exp(-jnp.abs(fl)))
    lane = lax.broadcasted_iota(jnp.int32, logf.shape, 1)
    logf = jnp.where(lane < HEADS, logf, 0.0)
    hi, mid, lo = _split3(logf)
    tri = tri_ref[...]
    c = (_dot(tri, hi) + _dot(tri, mid)) + _dot(tri, lo) + carry_ref[...]
    carry_ref[...] = c[tm - 1:tm, :]

    parts = jnp.concatenate(_split3(c * LOG2E), axis=1)
    aug = _dot(parts, place_ref[...]) + ones_ref[...]
    qa_ref[...] = aug[:, 0:LANES].astype(BF16)
    ka_ref[...] = aug[:, LANES:2 * LANES].astype(BF16)


def _inproj0(x, w_all, wv_t, bf_pad, conv_w, tri, place, ones_row):
    bsz, s_len, _ = x.shape
    tm = ROWS_IN0
    grid = (bsz, s_len // tm)
    row = lambda b, i: (b, i, 0)
    const2 = lambda b, i: (0, 0)
    out_bf = jax.ShapeDtypeStruct((bsz, s_len, FOX_WIDTH), BF16)
    out_aug = jax.ShapeDtypeStruct((bsz, s_len, LANES), BF16)
    return pl.pallas_call(
        _inproj0_kernel,
        grid=grid,
        in_specs=[
            pl.BlockSpec((None, tm, D_MODEL), row),
            pl.BlockSpec((D_MODEL, IN0_COLS), const2),
            pl.BlockSpec((FOX_WIDTH, D_MODEL), const2),
            pl.BlockSpec((1, F_PAD), const2),
            pl.BlockSpec((CONV_K, CONV_WIDTH), const2),
            pl.BlockSpec((tm, tm), const2),
            pl.BlockSpec((3 * LANES, 2 * LANES), const2),
            pl.BlockSpec((1, 2 * LANES), const2),
        ],
        out_specs=[
            pl.BlockSpec((None, tm, FOX_WIDTH), row),
            pl.BlockSpec((None, tm, FOX_WIDTH), row),
            pl.BlockSpec((None, HEADS // 2, None, LANES, tm), lambda b, i: (b, 0, i, 0, 0)),
            pl.BlockSpec((None, tm, CONV_WIDTH), row),
            pl.BlockSpec((None, tm, LANES), row),
            pl.BlockSpec((None, tm, LANES), row),
        ],
        out_shape=[
            out_bf, out_bf,
            jax.ShapeDtypeStruct((bsz, HEADS // 2, s_len // tm, LANES, tm), BF16),
            jax.ShapeDtypeStruct((bsz, s_len, CONV_WIDTH), BF16),
            out_aug, out_aug,
        ],
        scratch_shapes=[
            pltpu.VMEM((tm, D_MODEL), BF16),
            pltpu.VMEM((tm + 8, CONV_WIDTH), F32),
            pltpu.VMEM((1, LANES), F32),
        ],
        compiler_params=pltpu.CompilerParams(
            dimension_semantics=("arbitrary", "arbitrary"),
            vmem_limit_bytes=VMEM_LIMIT),
        name="inproj0",
    )(x, w_all, wv_t, bf_pad, conv_w, tri, place, ones_row)


def _attn_kernel(q_ref, qa_ref, k_ref, ka_ref, vt_ref, o_ref,
                 s_ref, mcur_ref, m_ref, l_ref, acc_ref):
    tq = q_ref.shape[0]
    tk = ATT_TK
    pair = pl.program_id(1)
    qi = pl.program_id(2)
    diag = (qi * tq) // tk
    nt_dims = (((1,), (1,)), ((), ()))

    lane = lax.broadcasted_iota(jnp.int32, (tq, LANES), 1)
    q2 = q_ref[...]
    qa = qa_ref[...]
    qfull = []
    for hh in range(2):
        head = 2 * pair + hh
        own_q = (lane >= hh * HEAD_DIM) & (lane < (hh + 1) * HEAD_DIM)
        own_a = (lane >= BIAS_LANES * head) & (lane < BIAS_LANES * (head + 1))
        qfull.append(jnp.concatenate(
            [jnp.where(own_q, q2, jnp.zeros_like(q2)),
             jnp.where(own_a, qa, jnp.zeros_like(qa))], axis=1))

    m_ref[...] = jnp.full(m_ref.shape, -jnp.inf, F32)
    l_ref[...] = jnp.zeros(l_ref.shape, F32)
    acc_ref[...] = jnp.zeros(acc_ref.shape, F32)

    def scores(j, slot):
        ks = pl.multiple_of(j * tk, tk)
        lhs = jnp.concatenate([k_ref[pl.ds(ks, tk), :], ka_ref[pl.ds(ks, tk), :]], axis=1)
        for hh in range(2):
            st = lax.dot_general(lhs, qfull[hh], nt_dims, preferred_element_type=F32)
            s_ref[slot, hh] = st
            mcur_ref[slot, hh] = jnp.max(st.reshape(tk // 8, 8, tq), axis=0)

    def softmax_pv(j, slot, masked):
        for hh in range(2):
            s2 = s_ref[slot, hh]
            if masked:
                key = j * tk + lax.broadcasted_iota(jnp.int32, (tk, tq), 0)
                qry = qi * tq + lax.broadcasted_iota(jnp.int32, (tk, tq), 1)
                s2 = jnp.where(key <= qry, s2, -jnp.inf)
            s3 = s2.reshape(tk // 8, 8, tq)
            mc8 = jnp.max(s3, axis=0) if masked else mcur_ref[slot, hh]
            m_prev = m_ref[hh]
            m_new = jnp.maximum(m_prev, jnp.max(mc8, axis=0, keepdims=True))
            alpha = jnp.exp2(m_prev - m_new)
            p3 = jnp.exp2(s3 - m_new[None])
            l_ref[hh] = alpha * l_ref[hh] + jnp.sum(p3, axis=0)
            m_ref[hh] = m_new
            p = p3.reshape(tk, tq).astype(BF16)
            rows = slice(hh * HEAD_DIM, (hh + 1) * HEAD_DIM)
            pv = _dot(vt_ref[j, rows, :], p)
            acc_ref[rows, :] = alpha[0:1, :] * acc_ref[rows, :] + pv

    scores(0, 0)

    def body(i, carry):
        j = 2 * i
        scores(j + 1, 1)
        softmax_pv(j, 0, False)
        scores(j + 2, 0)
        softmax_pv(j + 1, 1, False)
        return carry

    lax.fori_loop(0, diag // 2, body, 0)

    @pl.when(diag % 2 == 1)
    def _():
        scores(diag, 1)
        softmax_pv(diag - 1, 0, False)
        softmax_pv(diag, 1, True)

    @pl.when(diag % 2 == 0)
    def _():
        softmax_pv(diag, 0, True)

    halves = []
    for hh in range(2):
        rows = slice(hh * HEAD_DIM, (hh + 1) * HEAD_DIM)
        l_row = jnp.sum(l_ref[hh], axis=0, keepdims=True)
        halves.append(acc_ref[rows, :] / l_row)
    o_ref[...] = jnp.concatenate(halves, axis=0).T.astype(o_ref.dtype)


def _attention(q, qa, k, ka, vt):
    bsz, s_len, _ = q.shape
    tq, tk = ATT_TQ, ATT_TK
    n_pairs = HEADS // 2
    grid = (bsz, n_pairs, s_len // tq)
    return pl.pallas_call(
        _attn_kernel,
        grid=grid,
        in_specs=[
            pl.BlockSpec((None, tq, LANES), lambda b, p, i: (b, i, p)),
            pl.BlockSpec((None, tq, LANES), lambda b, p, i: (b, i, 0)),
            pl.BlockSpec((None, s_len, LANES), lambda b, p, i: (b, 0, p)),
            pl.BlockSpec((None, s_len, LANES), lambda b, p, i: (b, 0, 0)),
            pl.BlockSpec((None, None, s_len // tk, LANES, tk), lambda b, p, i: (b, p, 0, 0, 0)),
        ],
        out_specs=pl.BlockSpec((None, tq, LANES), lambda b, p, i: (b, i, p)),
        out_shape=jax.ShapeDtypeStruct((bsz, s_len, FOX_WIDTH), BF16),
        scratch_shapes=[
            pltpu.VMEM((2, 2, tk, tq), F32),
            pltpu.VMEM((2, 2, 8, tq), F32),
            pltpu.VMEM((2, 8, tq), F32),
            pltpu.VMEM((2, 8, tq), F32),
            pltpu.VMEM((2 * HEAD_DIM, tq), F32),
        ],
        compiler_params=pltpu.CompilerParams(
            dimension_semantics=("arbitrary", "arbitrary", "arbitrary"),
            vmem_limit_bytes=VMEM_LIMIT),
        name="fox_attention",
    )(q, qa, k, ka, vt)


def _outproj0_kernel(attn_ref, conv_ref, x_ref, w_ref, g_ref, b_ref, o_ref):
    m = (_dot(attn_ref[...], w_ref[0:FOX_WIDTH, :])
         + _dot(conv_ref[...], w_ref[FOX_WIDTH:FOX_WIDTH + CONV_WIDTH, :]))
    o_ref[...] = _layer_norm(ALPHA * x_ref[...] + m, g_ref[...], b_ref[...])


def _outproj0(attn, conv, x2, w_out, g, b):
    n = x2.shape[0]
    tm = ROWS_MLP
    row = lambda i: (i, 0)
    const = lambda i: (0, 0)
    return pl.pallas_call(
        _outproj0_kernel,
        grid=(n // tm,),
        in_specs=[
            pl.BlockSpec((tm, FOX_WIDTH), row),
            pl.BlockSpec((tm, CONV_WIDTH), row),
            pl.BlockSpec((tm, D_MODEL), row),
            pl.BlockSpec((FOX_WIDTH + CONV_WIDTH, D_MODEL), const),
            pl.BlockSpec((1, D_MODEL), const),
            pl.BlockSpec((1, D_MODEL), const),
        ],
        out_specs=pl.BlockSpec((tm, D_MODEL), row),
        out_shape=jax.ShapeDtypeStruct((n, D_MODEL), F32),
        compiler_params=pltpu.CompilerParams(
            dimension_semantics=("arbitrary",), vmem_limit_bytes=VMEM_LIMIT),
        name="outproj0_ln",
    )(attn, conv, x2, w_out, g, b)


def _ffn_kernel(x_ref, win_ref, wout_ref, g_ref, b_ref, o_ref, xb_ref, a_ref):
    xb_ref[...] = x_ref[...].astype(BF16)
    xb = xb_ref[...]
    for c in range(FFN_HIDDEN // FFN_CHUNK):
        lo = c * FFN_CHUNK
        gate = _dot(xb, win_ref[:, lo:lo + FFN_CHUNK])
        up = _dot(xb, win_ref[:, FFN_HIDDEN + lo:FFN_HIDDEN + lo + FFN_CHUNK])
        a_ref[:, lo:lo + FFN_CHUNK] = (gate * (1.0 / (1.0 + jnp.exp(-gate))) * up).astype(BF16)
    m = _dot(a_ref[...], wout_ref[...])
    o_ref[...] = _layer_norm(ALPHA * x_ref[...] + m, g_ref[...], b_ref[...])


def _ffn(x2, w_in, w_out, g, b):
    n = x2.shape[0]
    tm = ROWS_MLP
    row = lambda i: (i, 0)
    const = lambda i: (0, 0)
    return pl.pallas_call(
        _ffn_kernel,
        grid=(n // tm,),
        in_specs=[
            pl.BlockSpec((tm, D_MODEL), row),
            pl.BlockSpec((D_MODEL, 2 * FFN_HIDDEN), const),
            pl.BlockSpec((FFN_HIDDEN, D_MODEL), const),
            pl.BlockSpec((1, D_MODEL), const),
            pl.BlockSpec((1, D_MODEL), const),
        ],
        out_specs=pl.BlockSpec((tm, D_MODEL), row),
        out_shape=jax.ShapeDtypeStruct((n, D_MODEL), F32),
        scratch_shapes=[
            pltpu.VMEM((tm, D_MODEL), BF16),
            pltpu.VMEM((tm, FFN_HIDDEN), BF16),
        ],
        compiler_params=pltpu.CompilerParams(
            dimension_semantics=("arbitrary",), vmem_limit_bytes=VMEM_LIMIT),
        name="swiglu_ln",
    )(x2, w_in, w_out, g, b)


def _gmlp_kernel(x_ref, win_ref, vg_ref, vb_ref, ws_ref, bst_ref, wout_ref, g_ref, b_ref,
                 o_ref, xb_ref, u_ref, v_ref, vn_ref, t_ref):
    tm = x_ref.shape[0]
    n_blk = tm // GMLP_BLOCK
    xb_ref[...] = x_ref[...].astype(BF16)
    xb = xb_ref[...]
    for c in range(2 * D_MODEL // GMLP_IN_CHUNK):
        lo = c * GMLP_IN_CHUNK
        t = _dot(xb, win_ref[:, lo:lo + GMLP_IN_CHUNK])
        t = 0.5 * t * (1.0 + lax.erf(t * (2.0 ** -0.5)))
        if lo < D_MODEL:
            u_ref[:, lo:lo + GMLP_IN_CHUNK] = t
        else:
            v_ref[:, lo - D_MODEL:lo - D_MODEL + GMLP_IN_CHUNK] = t
    vn_ref[...] = _layer_norm(v_ref[...], vg_ref[...], vb_ref[...]).astype(BF16)

    pos_i = lax.broadcasted_iota(jnp.int32, (GMLP_BLOCK, GMLP_BLOCK), 0) // CHUNK
    pos_j = lax.broadcasted_iota(jnp.int32, (GMLP_BLOCK, GMLP_BLOCK), 1) // CHUNK
    causal = pos_j <= pos_i
    for g in range(GMLP_GROUPS):
        gl = slice(g * GMLP_BLOCK, (g + 1) * GMLP_BLOCK)
        w = jnp.where(causal, ws_ref[g], 0.0).astype(BF16)
        rhs = jnp.concatenate(
            [vn_ref[r * GMLP_BLOCK:(r + 1) * GMLP_BLOCK, gl] for r in range(n_blk)], axis=1)
        sg = _dot(w, rhs) + bst_ref[:, g:g + 1]
        for r in range(n_blk):
            rows = slice(r * GMLP_BLOCK, (r + 1) * GMLP_BLOCK)
            t_ref[rows, gl] = (u_ref[rows, gl]
                               * sg[:, r * GMLP_BLOCK:(r + 1) * GMLP_BLOCK]).astype(BF16)
    m = _dot(t_ref[...], wout_ref[...])
    o_ref[...] = _layer_norm(ALPHA * x_ref[...] + m, g_ref[...], b_ref[...])


def _gmlp(x2, w_in, v_g, v_b, w_s, b_s_t, w_out, g, b):
    n = x2.shape[0]
    tm = ROWS_MLP
    row = lambda i: (i, 0)
    const = lambda i: (0, 0)
    return pl.pallas_call(
        _gmlp_kernel,
        grid=(n // tm,),
        in_specs=[
            pl.BlockSpec((tm, D_MODEL), row),
            pl.BlockSpec((D_MODEL, 2 * D_MODEL), const),
            pl.BlockSpec((1, D_MODEL), const),
            pl.BlockSpec((1, D_MODEL), const),
            pl.BlockSpec((GMLP_GROUPS, GMLP_BLOCK, GMLP_BLOCK), lambda i: (0, 0, 0)),
            pl.BlockSpec((GMLP_BLOCK, GMLP_GROUPS), const),
            pl.BlockSpec((D_MODEL, D_MODEL), const),
            pl.BlockSpec((1, D_MODEL), const),
            pl.BlockSpec((1, D_MODEL), const),
        ],
        out_specs=pl.BlockSpec((tm, D_MODEL), row),
        out_shape=jax.ShapeDtypeStruct((n, D_MODEL), F32),
        scratch_shapes=[
            pltpu.VMEM((tm, D_MODEL), BF16),
            pltpu.VMEM((tm, D_MODEL), F32),
            pltpu.VMEM((tm, D_MODEL), F32),
            pltpu.VMEM((tm, D_MODEL), BF16),
            pltpu.VMEM((tm, D_MODEL), BF16),
        ],
        compiler_params=pltpu.CompilerParams(
            dimension_semantics=("arbitrary",), vmem_limit_bytes=VMEM_LIMIT),
        name="gmlp_ln",
    )(x2, w_in, v_g, v_b, w_s, b_s_t, w_out, g, b)


def _bias_placement():
    place = np.zeros((3 * LANES, 2 * LANES), np.float32)
    ones = np.zeros((1, 2 * LANES), np.float32)
    for h in range(HEADS):
        for part in range(3):
            place[part * LANES + h, BIAS_LANES * h + part] = 1.0
            place[part * LANES + h, LANES + BIAS_LANES * h + 3 + part] = -1.0
            ones[0, BIAS_LANES * h + 3 + part] = 1.0
            ones[0, LANES + BIAS_LANES * h + part] = 1.0
    return jnp.asarray(place, BF16), jnp.asarray(ones, F32)


def kernel(x, even_w_in, even_b_f, even_conv_w, even_w_out, odd_w_in, odd_v_ln_g, odd_v_ln_b,
           odd_w_s, odd_b_s, odd_w_out, mix_ln_g, mix_ln_b, ffn_w_in, ffn_w_out, ffn_ln_g,
           ffn_ln_b):
    bsz, s_len, _ = x.shape
    n = bsz * s_len
    row = lambda a: a.reshape(1, -1)

    w0 = even_w_in[0]
    v_lo, v_hi = 2 * FOX_WIDTH, 3 * FOX_WIDTH
    w_all = jnp.concatenate(
        [w0[:, :v_lo], w0[:, v_hi + HEADS:], w0[:, v_hi:v_hi + HEADS],
         jnp.zeros((D_MODEL, F_PAD - HEADS), F32)], axis=1).astype(BF16)
    wv_t = w0[:, v_lo:v_hi].T.astype(BF16)
    bf_pad = jnp.pad(even_b_f[0], (0, F_PAD - HEADS)).reshape(1, F_PAD)
    tri = jnp.tril(jnp.ones((ROWS_IN0, ROWS_IN0), BF16))
    place, ones_row = _bias_placement()

    q, k, vt, conv, qa, ka = _inproj0(x, w_all, wv_t, bf_pad, even_conv_w[0], tri, place,
                                      ones_row)
    attn = _attention(q, qa, k, ka, vt)
    x2 = x.reshape(n, D_MODEL)
    x2 = _outproj0(attn.reshape(n, FOX_WIDTH), conv.reshape(n, CONV_WIDTH), x2,
                   even_w_out[0].astype(BF16), row(mix_ln_g[0]), row(mix_ln_b[0]))
    x2 = _ffn(x2, ffn_w_in[0].astype(BF16), ffn_w_out[0].astype(BF16),
              row(ffn_ln_g[0]), row(ffn_ln_b[0]))
    x2 = _gmlp(x2, odd_w_in[0].astype(BF16), row(odd_v_ln_g[0]), row(odd_v_ln_b[0]),
               odd_w_s[0], odd_b_s[0].T, odd_w_out[0].astype(BF16),
               row(mix_ln_g[1]), row(mix_ln_b[1]))
    x2 = _ffn(x2, ffn_w_in[1].astype(BF16), ffn_w_out[1].astype(BF16),
              row(ffn_ln_g[1]), row(ffn_ln_b[1]))
    return x2.reshape(bsz, s_len, D_MODEL)
```

```python
import jax
import jax.numpy as jnp
import numpy as np
from jax import lax
from jax.experimental import pallas as pl
from jax.experimental.pallas import tpu as pltpu

D_MODEL = 1024
HEADS = 8
HEAD_DIM = 64
FOX_WIDTH = HEADS * HEAD_DIM
CONV_WIDTH = 512
CONV_K = 3
GMLP_BLOCK = 128
GMLP_GROUPS = 8
CHUNK = 64
FFN_HIDDEN = 2816
DEPTH = 2
ALPHA = (2.0 * DEPTH) ** 0.25
LN_EPS = 1e-5

LANES = 128
F_PAD = LANES
IN0_COLS = 2 * FOX_WIDTH + 3 * CONV_WIDTH + F_PAD
BIAS_LANES = 6
LOG2E = 1.4426950408889634
Q_SCALE = HEAD_DIM ** -0.5 * LOG2E
VMEM_LIMIT = 56 * 1024 * 1024

ROWS_IN0 = 512
ROWS_MLP = 512
ATT_TQ = 512
ATT_TK = 512
FFN_CHUNK = 256
GMLP_IN_CHUNK = 512

BF16 = jnp.bfloat16
F32 = jnp.float32


def _dot(a, b):
    return jnp.dot(a, b, preferred_element_type=F32)


def _layer_norm(y, g, b):
    mu = jnp.mean(y, axis=-1, keepdims=True)
    yc = y - mu
    var = jnp.mean(yc * yc, axis=-1, keepdims=True)
    return yc * lax.rsqrt(var + LN_EPS) * g + b


def _split3(a):
    hi = a.astype(BF16)
    r1 = a - hi.astype(F32)
    mid = r1.astype(BF16)
    lo = (r1 - mid.astype(F32)).astype(BF16)
    return hi, mid, lo


def _inproj0_kernel(x_ref, w_ref, wvt_ref, bf_ref, cw_ref, tri_ref, place_ref, ones_ref,
                    q_ref, k_ref, vt_ref, conv_ref, qa_ref, ka_ref,
                    xb_ref, zbuf_ref, carry_ref):
    tm = x_ref.shape[0]

    @pl.when(pl.program_id(1) == 0)
    def _():
        zbuf_ref[0:8, :] = jnp.zeros((8, CONV_WIDTH), F32)
        carry_ref[...] = jnp.zeros_like(carry_ref)

    xb_ref[...] = x_ref[...].astype(BF16)
    xb = xb_ref[...]
    w = FOX_WIDTH
    q_ref[...] = (_dot(xb, w_ref[:, 0:w]) * Q_SCALE).astype(BF16)
    k_ref[...] = _dot(xb, w_ref[:, w:2 * w]).astype(BF16)
    vt = lax.dot_general(wvt_ref[...], xb, (((1,), (1,)), ((), ())),
                         preferred_element_type=F32)
    vt_ref[...] = vt.astype(BF16).reshape(HEADS // 2, LANES, tm)

    o = 2 * w
    cg = _dot(xb, w_ref[:, o + CONV_WIDTH:o + 2 * CONV_WIDTH])
    hh = _dot(xb, w_ref[:, o + 2 * CONV_WIDTH:o + 3 * CONV_WIDTH])
    z = cg * hh
    zbuf_ref[8:8 + tm, :] = z
    y = (cw_ref[0:1, :] * zbuf_ref[6:6 + tm, :]
         + cw_ref[1:2, :] * zbuf_ref[7:7 + tm, :]
         + cw_ref[2:3, :] * z)
    bg = _dot(xb, w_ref[:, o:o + CONV_WIDTH])
    conv_ref[...] = (bg * y).astype(BF16)
    zbuf_ref[0:8, :] = zbuf_ref[tm:tm + 8, :]

    fl = _dot(xb, w_ref[:, o + 3 * CONV_WIDTH:o + 3 * CONV_WIDTH + F_PAD]) + bf_ref[...]
    logf = jnp.minimum(fl, 0.0) - jnp.log1p(jnp.exp(-jnp.abs(fl)))
    lane = lax.broadcasted_iota(jnp.int32, logf.shape, 1)
    logf = jnp.where(lane < HEADS, logf, 0.0)
    hi, mid, lo = _split3(logf)
    tri = tri_ref[...]
    c = (_dot(tri, hi) + _dot(tri, mid)) + _dot(tri, lo) + carry_ref[...]
    carry_ref[...] = c[tm - 1:tm, :]

    parts = jnp.concatenate(_split3(c * LOG2E), axis=1)
    aug = _dot(parts, place_ref[...]) + ones_ref[...]
    qa_ref[...] = aug[:, 0:LANES].astype(BF16)
    ka_ref[...] = aug[:, LANES:2 * LANES].astype(BF16)


def _inproj0(x, w_all, wv_t, bf_pad, conv_w, tri, place, ones_row):
    bsz, s_len, _ = x.shape
    tm = ROWS_IN0
    grid = (bsz, s_len // tm)
    row = lambda b, i: (b, i, 0)
    const2 = lambda b, i: (0, 0)
    out_bf = jax.ShapeDtypeStruct((bsz, s_len, FOX_WIDTH), BF16)
    out_aug = jax.ShapeDtypeStruct((bsz, s_len, LANES), BF16)
    return pl.pallas_call(
        _inproj0_kernel,
        grid=grid,
        in_specs=[
            pl.BlockSpec((None, tm, D_MODEL), row),
            pl.BlockSpec((D_MODEL, IN0_COLS), const2),
            pl.BlockSpec((FOX_WIDTH, D_MODEL), const2),
            pl.BlockSpec((1, F_PAD), const2),
            pl.BlockSpec((CONV_K, CONV_WIDTH), const2),
            pl.BlockSpec((tm, tm), const2),
            pl.BlockSpec((3 * LANES, 2 * LANES), const2),
            pl.BlockSpec((1, 2 * LANES), const2),
        ],
        out_specs=[
            pl.BlockSpec((None, tm, FOX_WIDTH), row),
            pl.BlockSpec((None, tm, FOX_WIDTH), row),
            pl.BlockSpec((None, HEADS // 2, None, LANES, tm), lambda b, i: (b, 0, i, 0, 0)),
            pl.BlockSpec((None, tm, CONV_WIDTH), row),
            pl.BlockSpec((None, tm, LANES), row),
            pl.BlockSpec((None, tm, LANES), row),
        ],
        out_shape=[
            out_bf, out_bf,
            jax.ShapeDtypeStruct((bsz, HEADS // 2, s_len // tm, LANES, tm), BF16),
            jax.ShapeDtypeStruct((bsz, s_len, CONV_WIDTH), BF16),
            out_aug, out_aug,
        ],
        scratch_shapes=[
            pltpu.VMEM((tm, D_MODEL), BF16),
            pltpu.VMEM((tm + 8, CONV_WIDTH), F32),
            pltpu.VMEM((1, LANES), F32),
        ],
        compiler_params=pltpu.CompilerParams(
            dimension_semantics=("arbitrary", "arbitrary"),
            vmem_limit_bytes=VMEM_LIMIT),
        name="inproj0",
    )(x, w_all, wv_t, bf_pad, conv_w, tri, place, ones_row)


def _attn_kernel(q_ref, qa_ref, k_ref, ka_ref, vt_ref, o_ref,
                 s_ref, mcur_ref, m_ref, l_ref, acc_ref):
    tq = q_ref.shape[0]
    tk = ATT_TK
    pair = pl.program_id(1)
    qi = pl.program_id(2)
    diag = (qi * tq) // tk
    nt_dims = (((1,), (1,)), ((), ()))

    lane = lax.broadcasted_iota(jnp.int32, (tq, LANES), 1)
    q2 = q_ref[...]
    qa = qa_ref[...]
    qfull = []
    for hh in range(2):
        head = 2 * pair + hh
        own_q = (lane >= hh * HEAD_DIM) & (lane < (hh + 1) * HEAD_DIM)
        own_a = (lane >= BIAS_LANES * head) & (lane < BIAS_LANES * (head + 1))
        qfull.append(jnp.concatenate(
            [jnp.where(own_q, q2, jnp.zeros_like(q2)),
             jnp.where(own_a, qa, jnp.zeros_like(qa))], axis=1))

    m_ref[...] = jnp.full(m_ref.shape, -jnp.inf, F32)
    l_ref[...] = jnp.zeros(l_ref.shape, F32)
    acc_ref[...] = jnp.zeros(acc_ref.shape, F32)

    def scores(j, slot, masked):
        ks = pl.multiple_of(j * tk, tk)
        lhs = jnp.concatenate([k_ref[pl.ds(ks, tk), :], ka_ref[pl.ds(ks, tk), :]], axis=1)
        for hh in range(2):
            st = lax.dot_general(lhs, qfull[hh], nt_dims, preferred_element_type=F32)
            if masked:
                key = lax.broadcasted_iota(jnp.int32, (tk, tq), 0)
                qry = lax.broadcasted_iota(jnp.int32, (tk, tq), 1)
                st = jnp.where(key <= qry, st, -jnp.inf)
            s_ref[slot, hh] = st
            mcur_ref[slot, hh] = jnp.max(st.reshape(tk // 8, 8, tq), axis=0)

    def softmax_pv(j, slot):
        for hh in range(2):
            s3 = s_ref[slot, hh].reshape(tk // 8, 8, tq)
            m_prev = m_ref[hh]
            m_new = jnp.maximum(m_prev, jnp.max(mcur_ref[slot, hh], axis=0, keepdims=True))
            alpha = jnp.exp2(m_prev - m_new)
            p3 = jnp.exp2(s3 - m_new[None])
            l_ref[hh] = alpha * l_ref[hh] + jnp.sum(p3, axis=0)
            m_ref[hh] = m_new
            p = p3.reshape(tk, tq).astype(BF16)
            rows = slice(hh * HEAD_DIM, (hh + 1) * HEAD_DIM)
            pv = _dot(vt_ref[j, rows, :], p)
            acc_ref[rows, :] = alpha[0:1, :] * acc_ref[rows, :] + pv

    def tile_at(n):
        return jnp.where(n == 0, diag, n - 1)

    scores(diag, 0, True)

    def body(i, carry):
        n = 2 * i
        scores(n, 1, False)
        softmax_pv(tile_at(n), 0)
        scores(n + 1, 0, False)
        softmax_pv(n, 1)
        return carry

    lax.fori_loop(0, diag // 2, body, 0)

    @pl.when(diag % 2 == 1)
    def _():
        scores(diag - 1, 1, False)
        softmax_pv(tile_at(diag - 1), 0)
        softmax_pv(diag - 1, 1)

    @pl.when(diag % 2 == 0)
    def _():
        softmax_pv(tile_at(diag), 0)

    halves = []
    for hh in range(2):
        rows = slice(hh * HEAD_DIM, (hh + 1) * HEAD_DIM)
        l_row = jnp.sum(l_ref[hh], axis=0, keepdims=True)
        halves.append(acc_ref[rows, :] / l_row)
    o_ref[...] = jnp.concatenate(halves, axis=0).T.astype(o_ref.dtype)


def _attention(q, qa, k, ka, vt):
    bsz, s_len, _ = q.shape
    tq, tk = ATT_TQ, ATT_TK
    assert tq == tk == ROWS_IN0
    n_pairs = HEADS // 2
    grid = (bsz, n_pairs, s_len // tq)
    return pl.pallas_call(
        _attn_kernel,
        grid=grid,
        in_specs=[
            pl.BlockSpec((None, tq, LANES), lambda b, p, i: (b, i, p)),
            pl.BlockSpec((None, tq, LANES), lambda b, p, i: (b, i, 0)),
            pl.BlockSpec((None, s_len, LANES), lambda b, p, i: (b, 0, p)),
            pl.BlockSpec((None, s_len, LANES), lambda b, p, i: (b, 0, 0)),
            pl.BlockSpec((None, None, s_len // tk, LANES, tk), lambda b, p, i: (b, p, 0, 0, 0)),
        ],
        out_specs=pl.BlockSpec((None, tq, LANES), lambda b, p, i: (b, i, p)),
        out_shape=jax.ShapeDtypeStruct((bsz, s_len, FOX_WIDTH), BF16),
        scratch_shapes=[
            pltpu.VMEM((2, 2, tk, tq), F32),
            pltpu.VMEM((2, 2, 8, tq), F32),
            pltpu.VMEM((2, 8, tq), F32),
            pltpu.VMEM((2, 8, tq), F32),
            pltpu.VMEM((2 * HEAD_DIM, tq), F32),
        ],
        compiler_params=pltpu.CompilerParams(
            dimension_semantics=("arbitrary", "arbitrary", "arbitrary"),
            vmem_limit_bytes=VMEM_LIMIT),
        name="fox_attention",
    )(q, qa, k, ka, vt)


def _outproj0_kernel(attn_ref, conv_ref, x_ref, w_ref, g_ref, b_ref, o_ref):
    m = (_dot(attn_ref[...], w_ref[0:FOX_WIDTH, :])
         + _dot(conv_ref[...], w_ref[FOX_WIDTH:FOX_WIDTH + CONV_WIDTH, :]))
    o_ref[...] = _layer_norm(ALPHA * x_ref[...] + m, g_ref[...], b_ref[...])


def _outproj0(attn, conv, x2, w_out, g, b):
    n = x2.shape[0]
    tm = ROWS_MLP
    row = lambda i: (i, 0)
    const = lambda i: (0, 0)
    return pl.pallas_call(
        _outproj0_kernel,
        grid=(n // tm,),
        in_specs=[
            pl.BlockSpec((tm, FOX_WIDTH), row),
            pl.BlockSpec((tm, CONV_WIDTH), row),
            pl.BlockSpec((tm, D_MODEL), row),
            pl.BlockSpec((FOX_WIDTH + CONV_WIDTH, D_MODEL), const),
            pl.BlockSpec((1, D_MODEL), const),
            pl.BlockSpec((1, D_MODEL), const),
        ],
        out_specs=pl.BlockSpec((tm, D_MODEL), row),
        out_shape=jax.ShapeDtypeStruct((n, D_MODEL), F32),
        compiler_params=pltpu.CompilerParams(
            dimension_semantics=("arbitrary",), vmem_limit_bytes=VMEM_LIMIT),
        name="outproj0_ln",
    )(attn, conv, x2, w_out, g, b)


def _ffn_kernel(x_ref, win_ref, wout_ref, g_ref, b_ref, o_ref, xb_ref, a_ref):
    xb_ref[...] = x_ref[...].astype(BF16)
    xb = xb_ref[...]
    for c in range(FFN_HIDDEN // FFN_CHUNK):
        lo = c * FFN_CHUNK
        gate = _dot(xb, win_ref[:, lo:lo + FFN_CHUNK])
        up = _dot(xb, win_ref[:, FFN_HIDDEN + lo:FFN_HIDDEN + lo + FFN_CHUNK])
        a_ref[:, lo:lo + FFN_CHUNK] = (gate * (1.0 / (1.0 + jnp.exp(-gate))) * up).astype(BF16)
    m = _dot(a_ref[...], wout_ref[...])
    o_ref[...] = _layer_norm(ALPHA * x_ref[...] + m, g_ref[...], b_ref[...])


def _ffn(x2, w_in, w_out, g, b):
    n = x2.shape[0]
    tm = ROWS_MLP
    row = lambda i: (i, 0)
    const = lambda i: (0, 0)
    return pl.pallas_call(
        _ffn_kernel,
        grid=(n // tm,),
        in_specs=[
            pl.BlockSpec((tm, D_MODEL), row),
            pl.BlockSpec((D_MODEL, 2 * FFN_HIDDEN), const),
            pl.BlockSpec((FFN_HIDDEN, D_MODEL), const),
            pl.BlockSpec((1, D_MODEL), const),
            pl.BlockSpec((1, D_MODEL), const),
        ],
        out_specs=pl.BlockSpec((tm, D_MODEL), row),
        out_shape=jax.ShapeDtypeStruct((n, D_MODEL), F32),
        scratch_shapes=[
            pltpu.VMEM((tm, D_MODEL), BF16),
            pltpu.VMEM((tm, FFN_HIDDEN), BF16),
        ],
        compiler_params=pltpu.CompilerParams(
            dimension_semantics=("arbitrary",), vmem_limit_bytes=VMEM_LIMIT),
        name="swiglu_ln",
    )(x2, w_in, w_out, g, b)


def _gmlp_kernel(x_ref, win_ref, vg_ref, vb_ref, ws_ref, bst_ref, wout_ref, g_ref, b_ref,
                 o_ref, xb_ref, u_ref, v_ref, vn_ref, t_ref):
    tm = x_ref.shape[0]
    n_blk = tm // GMLP_BLOCK
    xb_ref[...] = x_ref[...].astype(BF16)
    xb = xb_ref[...]
    for c in range(2 * D_MODEL // GMLP_IN_CHUNK):
        lo = c * GMLP_IN_CHUNK
        t = _dot(xb, win_ref[:, lo:lo + GMLP_IN_CHUNK])
        t = 0.5 * t * (1.0 + lax.erf(t * (2.0 ** -0.5)))
        if lo < D_MODEL:
            u_ref[:, lo:lo + GMLP_IN_CHUNK] = t
        else:
            v_ref[:, lo - D_MODEL:lo - D_MODEL + GMLP_IN_CHUNK] = t
    vn_ref[...] = _layer_norm(v_ref[...], vg_ref[...], vb_ref[...]).astype(BF16)

    pos_i = lax.broadcasted_iota(jnp.int32, (GMLP_BLOCK, GMLP_BLOCK), 0) // CHUNK
    pos_j = lax.broadcasted_iota(jnp.int32, (GMLP_BLOCK, GMLP_BLOCK), 1) // CHUNK
    causal = pos_j <= pos_i
    for g in range(GMLP_GROUPS):
        gl = slice(g * GMLP_BLOCK, (g + 1) * GMLP_BLOCK)
        w = jnp.where(causal, ws_ref[g], 0.0).astype(BF16)
        rhs = jnp.concatenate(
            [vn_ref[r * GMLP_BLOCK:(r + 1) * GMLP_BLOCK, gl] for r in range(n_blk)], axis=1)
        sg = _dot(w, rhs) + bst_ref[:, g:g + 1]
        for r in range(n_blk):
            rows = slice(r * GMLP_BLOCK, (r + 1) * GMLP_BLOCK)
            t_ref[rows, gl] = (u_ref[rows, gl]
                               * sg[:, r * GMLP_BLOCK:(r + 1) * GMLP_BLOCK]).astype(BF16)
    m = _dot(t_ref[...], wout_ref[...])
    o_ref[...] = _layer_norm(ALPHA * x_ref[...] + m, g_ref[...], b_ref[...])


def _gmlp(x2, w_in, v_g, v_b, w_s, b_s_t, w_out, g, b):
    n = x2.shape[0]
    tm = ROWS_MLP
    row = lambda i: (i, 0)
    const = lambda i: (0, 0)
    return pl.pallas_call(
        _gmlp_kernel,
        grid=(n // tm,),
        in_specs=[
            pl.BlockSpec((tm, D_MODEL), row),
            pl.BlockSpec((D_MODEL, 2 * D_MODEL), const),
            pl.BlockSpec((1, D_MODEL), const),
            pl.BlockSpec((1, D_MODEL), const),
            pl.BlockSpec((GMLP_GROUPS, GMLP_BLOCK, GMLP_BLOCK), lambda i: (0, 0, 0)),
            pl.BlockSpec((GMLP_BLOCK, GMLP_GROUPS), const),
            pl.BlockSpec((D_MODEL, D_MODEL), const),
            pl.BlockSpec((1, D_MODEL), const),
            pl.BlockSpec((1, D_MODEL), const),
        ],
        out_specs=pl.BlockSpec((tm, D_MODEL), row),
        out_shape=jax.ShapeDtypeStruct((n, D_MODEL), F32),
        scratch_shapes=[
            pltpu.VMEM((tm, D_MODEL), BF16),
            pltpu.VMEM((tm, D_MODEL), F32),
            pltpu.VMEM((tm, D_MODEL), F32),
            pltpu.VMEM((tm, D_MODEL), BF16),
            pltpu.VMEM((tm, D_MODEL), BF16),
        ],
        compiler_params=pltpu.CompilerParams(
            dimension_semantics=("arbitrary",), vmem_limit_bytes=VMEM_LIMIT),
        name="gmlp_ln",
    )(x2, w_in, v_g, v_b, w_s, b_s_t, w_out, g, b)


def _bias_placement():
    place = np.zeros((3 * LANES, 2 * LANES), np.float32)
    ones = np.zeros((1, 2 * LANES), np.float32)
    for h in range(HEADS):
        for part in range(3):
            place[part * LANES + h, BIAS_LANES * h + part] = 1.0
            place[part * LANES + h, LANES + BIAS_LANES * h + 3 + part] = -1.0
            ones[0, BIAS_LANES * h + 3 + part] = 1.0
            ones[0, LANES + BIAS_LANES * h + part] = 1.0
    return jnp.asarray(place, BF16), jnp.asarray(ones, F32)


def kernel(x, even_w_in, even_b_f, even_conv_w, even_w_out, odd_w_in, odd_v_ln_g, odd_v_ln_b,
           odd_w_s, odd_b_s, odd_w_out, mix_ln_g, mix_ln_b, ffn_w_in, ffn_w_out, ffn_ln_g,
           ffn_ln_b):
    bsz, s_len, _ = x.shape
    n = bsz * s_len
    row = lambda a: a.reshape(1, -1)

    w0 = even_w_in[0]
    v_lo, v_hi = 2 * FOX_WIDTH, 3 * FOX_WIDTH
    w_all = jnp.concatenate(
        [w0[:, :v_lo], w0[:, v_hi + HEADS:], w0[:, v_hi:v_hi + HEADS],
         jnp.zeros((D_MODEL, F_PAD - HEADS), F32)], axis=1).astype(BF16)
    wv_t = w0[:, v_lo:v_hi].T.astype(BF16)
    bf_pad = jnp.pad(even_b_f[0], (0, F_PAD - HEADS)).reshape(1, F_PAD)
    tri = jnp.tril(jnp.ones((ROWS_IN0, ROWS_IN0), BF16))
    place, ones_row = _bias_placement()

    q, k, vt, conv, qa, ka = _inproj0(x, w_all, wv_t, bf_pad, even_conv_w[0], tri, place,
                                      ones_row)
    attn = _attention(q, qa, k, ka, vt)
    x2 = x.reshape(n, D_MODEL)
    x2 = _outproj0(attn.reshape(n, FOX_WIDTH), conv.reshape(n, CONV_WIDTH), x2,
                   even_w_out[0].astype(BF16), row(mix_ln_g[0]), row(mix_ln_b[0]))
    x2 = _ffn(x2, ffn_w_in[0].astype(BF16), ffn_w_out[0].astype(BF16),
              row(ffn_ln_g[0]), row(ffn_ln_b[0]))
    x2 = _gmlp(x2, odd_w_in[0].astype(BF16), row(odd_v_ln_g[0]), row(odd_v_ln_b[0]),
               odd_w_s[0], odd_b_s[0].T, odd_w_out[0].astype(BF16),
               row(mix_ln_g[1]), row(mix_ln_b[1]))
    x2 = _ffn(x2, ffn_w_in[1].astype(BF16), ffn_w_out[1].astype(BF16),
              row(ffn_ln_g[1]), row(ffn_ln_b[1]))
    return x2.reshape(bsz, s_len, D_MODEL)
```

```python
import jax
import jax.numpy as jnp
import numpy as np
from jax import lax
from jax.experimental import pallas as pl
from jax.experimental.pallas import tpu as pltpu

D_MODEL = 1024
HEADS = 8
HEAD_DIM = 64
FOX_WIDTH = HEADS * HEAD_DIM
CONV_WIDTH = 512
CONV_K = 3
GMLP_BLOCK = 128
GMLP_GROUPS = 8
CHUNK = 64
FFN_HIDDEN = 2816
DEPTH = 2
ALPHA = (2.0 * DEPTH) ** 0.25
LN_EPS = 1e-5

LANES = 128
F_PAD = LANES
IN0_COLS = 2 * FOX_WIDTH + 3 * CONV_WIDTH + F_PAD
BIAS_LANES = 6
LOG2E = 1.4426950408889634
Q_SCALE = HEAD_DIM ** -0.5 * LOG2E
VMEM_LIMIT = 56 * 1024 * 1024

ROWS_IN0 = 512
ROWS_MLP = 512
ATT_TQ = 512
ATT_TK = 512
ATT_UNIT = 256
FFN_CHUNK = 256
GMLP_IN_CHUNK = 512

BF16 = jnp.bfloat16
F32 = jnp.float32


def _dot(a, b):
    return jnp.dot(a, b, preferred_element_type=F32)


def _layer_norm(y, g, b):
    mu = jnp.mean(y, axis=-1, keepdims=True)
    yc = y - mu
    var = jnp.mean(yc * yc, axis=-1, keepdims=True)
    return yc * lax.rsqrt(var + LN_EPS) * g + b


def _split3(a):
    hi = a.astype(BF16)
    r1 = a - hi.astype(F32)
    mid = r1.astype(BF16)
    lo = (r1 - mid.astype(F32)).astype(BF16)
    return hi, mid, lo


def _inproj0_kernel(x_ref, w_ref, wvt_ref, bf_ref, cw_ref, tri_ref, place_ref, ones_ref,
                    q_ref, k_ref, vt_ref, conv_ref, qa_ref, ka_ref,
                    xb_ref, zbuf_ref, carry_ref):
    tm = x_ref.shape[0]

    @pl.when(pl.program_id(1) == 0)
    def _():
        zbuf_ref[0:8, :] = jnp.zeros((8, CONV_WIDTH), F32)
        carry_ref[...] = jnp.zeros_like(carry_ref)

    xb_ref[...] = x_ref[...].astype(BF16)
    xb = xb_ref[...]
    w = FOX_WIDTH
    q_ref[...] = (_dot(xb, w_ref[:, 0:w]) * Q_SCALE).astype(BF16)
    k_ref[...] = _dot(xb, w_ref[:, w:2 * w]).astype(BF16)
    vt = lax.dot_general(wvt_ref[...], xb, (((1,), (1,)), ((), ())),
                         preferred_element_type=F32)
    vt_ref[...] = vt.astype(BF16).reshape(HEADS // 2, LANES, tm)

    o = 2 * w
    cg = _dot(xb, w_ref[:, o + CONV_WIDTH:o + 2 * CONV_WIDTH])
    hh = _dot(xb, w_ref[:, o + 2 * CONV_WIDTH:o + 3 * CONV_WIDTH])
    z = cg * hh
    zbuf_ref[8:8 + tm, :] = z
    y = (cw_ref[0:1, :] * zbuf_ref[6:6 + tm, :]
         + cw_ref[1:2, :] * zbuf_ref[7:7 + tm, :]
         + cw_ref[2:3, :] * z)
    bg = _dot(xb, w_ref[:, o:o + CONV_WIDTH])
    conv_ref[...] = (bg * y).astype(BF16)
    zbuf_ref[0:8, :] = zbuf_ref[tm:tm + 8, :]

    fl = _dot(xb, w_ref[:, o + 3 * CONV_WIDTH:o + 3 * CONV_WIDTH + F_PAD]) + bf_ref[...]
    logf = jnp.minimum(fl, 0.0) - jnp.log1p(jnp.exp(-jnp.abs(fl)))
    lane = lax.broadcasted_iota(jnp.int32, logf.shape, 1)
    logf = jnp.where(lane < HEADS, logf, 0.0)
    hi, mid, lo = _split3(logf)
    tri = tri_ref[...]
    c = (_dot(tri, hi) + _dot(tri, mid)) + _dot(tri, lo) + carry_ref[...]
    carry_ref[...] = c[tm - 1:tm, :]

    parts = jnp.concatenate(_split3(c * LOG2E), axis=1)
    aug = _dot(parts, place_ref[...]) + ones_ref[...]
    qa_ref[...] = aug[:, 0:LANES].astype(BF16)
    ka_ref[...] = aug[:, LANES:2 * LANES].astype(BF16)


def _inproj0(x, w_all, wv_t, bf_pad, conv_w, tri, place, ones_row):
    bsz, s_len, _ = x.shape
    tm = ROWS_IN0
    grid = (bsz, s_len // tm)
    row = lambda b, i: (b, i, 0)
    const2 = lambda b, i: (0, 0)
    out_bf = jax.ShapeDtypeStruct((bsz, s_len, FOX_WIDTH), BF16)
    out_aug = jax.ShapeDtypeStruct((bsz, s_len, LANES), BF16)
    return pl.pallas_call(
        _inproj0_kernel,
        grid=grid,
        in_specs=[
            pl.BlockSpec((None, tm, D_MODEL), row),
            pl.BlockSpec((D_MODEL, IN0_COLS), const2),
            pl.BlockSpec((FOX_WIDTH, D_MODEL), const2),
            pl.BlockSpec((1, F_PAD), const2),
            pl.BlockSpec((CONV_K, CONV_WIDTH), const2),
            pl.BlockSpec((tm, tm), const2),
            pl.BlockSpec((3 * LANES, 2 * LANES), const2),
            pl.BlockSpec((1, 2 * LANES), const2),
        ],
        out_specs=[
            pl.BlockSpec((None, tm, FOX_WIDTH), row),
            pl.BlockSpec((None, tm, FOX_WIDTH), row),
            pl.BlockSpec((None, HEADS // 2, None, LANES, tm), lambda b, i: (b, 0, i, 0, 0)),
            pl.BlockSpec((None, tm, CONV_WIDTH), row),
            pl.BlockSpec((None, tm, LANES), row),
            pl.BlockSpec((None, tm, LANES), row),
        ],
        out_shape=[
            out_bf, out_bf,
            jax.ShapeDtypeStruct((bsz, HEADS // 2, s_len // tm, LANES, tm), BF16),
            jax.ShapeDtypeStruct((bsz, s_len, CONV_WIDTH), BF16),
            out_aug, out_aug,
        ],
        scratch_shapes=[
            pltpu.VMEM((tm, D_MODEL), BF16),
            pltpu.VMEM((tm + 8, CONV_WIDTH), F32),
            pltpu.VMEM((1, LANES), F32),
        ],
        compiler_params=pltpu.CompilerParams(
            dimension_semantics=("arbitrary", "arbitrary"),
            vmem_limit_bytes=VMEM_LIMIT),
        name="inproj0",
    )(x, w_all, wv_t, bf_pad, conv_w, tri, place, ones_row)


def _attn_kernel(q_ref, qa_ref, k_ref, ka_ref, vt_ref, o_ref,
                 qt_ref, s_ref, mcur_ref, m_ref, l_ref, acc_ref):
    tq = q_ref.shape[0]
    tk = ATT_TK
    pair = pl.program_id(1)
    qi = pl.program_id(2)
    diag = (qi * tq) // tk

    lane = lax.broadcasted_iota(jnp.int32, (tq, LANES), 1)
    q2 = q_ref[...]
    qa = qa_ref[...]
    for hh in range(2):
        head = 2 * pair + hh
        own_q = (lane >= hh * HEAD_DIM) & (lane < (hh + 1) * HEAD_DIM)
        own_a = (lane >= BIAS_LANES * head) & (lane < BIAS_LANES * (head + 1))
        qfull = jnp.concatenate(
            [jnp.where(own_q, q2, jnp.zeros_like(q2)),
             jnp.where(own_a, qa, jnp.zeros_like(qa))], axis=1)
        qt_ref[hh] = qfull.astype(F32).T.astype(BF16)

    m_ref[...] = jnp.full(m_ref.shape, -jnp.inf, F32)
    l_ref[...] = jnp.zeros(l_ref.shape, F32)
    acc_ref[...] = jnp.zeros(acc_ref.shape, F32)

    units = [(hh, qh) for hh in range(2) for qh in range(tq // ATT_UNIT)]

    def scores(j, slot, masked, u):
        hh, qh = units[u]
        ks = pl.multiple_of(j * tk, tk)
        lhs = jnp.concatenate([k_ref[pl.ds(ks, tk), :], ka_ref[pl.ds(ks, tk), :]], axis=1)
        st = _dot(lhs, qt_ref[hh, :, qh * ATT_UNIT:(qh + 1) * ATT_UNIT])
        if masked:
            key = lax.broadcasted_iota(jnp.int32, (tk, ATT_UNIT), 0)
            qry = lax.broadcasted_iota(jnp.int32, (tk, ATT_UNIT), 1) + qh * ATT_UNIT
            st = jnp.where(key <= qry, st, -jnp.inf)
        s_ref[slot, u] = st
        mcur_ref[slot, u] = jnp.max(st.reshape(tk // 8, 8, ATT_UNIT), axis=0)

    def softmax_pv(j, slot, u):
        hh, _ = units[u]
        s3 = s_ref[slot, u].reshape(tk // 8, 8, ATT_UNIT)
        m_prev = m_ref[u]
        m_new = jnp.maximum(m_prev, jnp.max(mcur_ref[slot, u], axis=0, keepdims=True))
        alpha = jnp.exp2(m_prev - m_new)
        m_ref[u] = m_new
        p = jnp.exp2((s3 - m_new[None]).reshape(tk, ATT_UNIT).astype(BF16))
        v_ones = jnp.concatenate(
            [vt_ref[j, hh * HEAD_DIM:(hh + 1) * HEAD_DIM, :], jnp.ones((16, tk), BF16)], axis=0)
        pv = _dot(v_ones, p)
        l_ref[u] = alpha * l_ref[u] + pv[HEAD_DIM:HEAD_DIM + 8, :]
        acc_ref[u] = alpha[0:1, :] * acc_ref[u] + pv[0:HEAD_DIM, :]

    def tile_at(n):
        return jnp.where(n == 0, diag, n - 1)

    def step(score_tile, score_slot, pv_tile, pv_slot):
        for u in range(len(units)):
            if score_tile is not None:
                scores(score_tile, score_slot, False, u)
            softmax_pv(pv_tile, pv_slot, u)

    for u in range(len(units)):
        scores(diag, 0, True, u)

    def body(i, carry):
        n = 2 * i
        step(n, 1, tile_at(n), 0)
        step(n + 1, 0, n, 1)
        return carry

    lax.fori_loop(0, diag // 2, body, 0)

    @pl.when(diag % 2 == 1)
    def _():
        step(diag - 1, 1, tile_at(diag - 1), 0)
        step(None, None, diag - 1, 1)

    @pl.when(diag % 2 == 0)
    def _():
        step(None, None, tile_at(diag), 0)

    for qh in range(tq // ATT_UNIT):
        halves = []
        for hh in range(2):
            u = units.index((hh, qh))
            halves.append(acc_ref[u] / l_ref[u][0:1, :])
        o_ref[qh * ATT_UNIT:(qh + 1) * ATT_UNIT, :] = (
            jnp.concatenate(halves, axis=0).T.astype(o_ref.dtype))


def _attention(q, qa, k, ka, vt):
    bsz, s_len, _ = q.shape
    tq, tk = ATT_TQ, ATT_TK
    assert tq == tk == ROWS_IN0
    n_units = 2 * (tq // ATT_UNIT)
    n_pairs = HEADS // 2
    grid = (bsz, n_pairs, s_len // tq)
    return pl.pallas_call(
        _attn_kernel,
        grid=grid,
        in_specs=[
            pl.BlockSpec((None, tq, LANES), lambda b, p, i: (b, i, p)),
            pl.BlockSpec((None, tq, LANES), lambda b, p, i: (b, i, 0)),
            pl.BlockSpec((None, s_len, LANES), lambda b, p, i: (b, 0, p)),
            pl.BlockSpec((None, s_len, LANES), lambda b, p, i: (b, 0, 0)),
            pl.BlockSpec((None, None, s_len // tk, LANES, tk), lambda b, p, i: (b, p, 0, 0, 0)),
        ],
        out_specs=pl.BlockSpec((None, tq, LANES), lambda b, p, i: (b, i, p)),
        out_shape=jax.ShapeDtypeStruct((bsz, s_len, FOX_WIDTH), BF16),
        scratch_shapes=[
            pltpu.VMEM((2, 2 * LANES, tq), BF16),
            pltpu.VMEM((2, n_units, tk, ATT_UNIT), F32),
            pltpu.VMEM((2, n_units, 8, ATT_UNIT), F32),
            pltpu.VMEM((n_units, 8, ATT_UNIT), F32),
            pltpu.VMEM((n_units, 8, ATT_UNIT), F32),
            pltpu.VMEM((n_units, HEAD_DIM, ATT_UNIT), F32),
        ],
        compiler_params=pltpu.CompilerParams(
            dimension_semantics=("arbitrary", "arbitrary", "arbitrary"),
            vmem_limit_bytes=VMEM_LIMIT),
        name="fox_attention",
    )(q, qa, k, ka, vt)


def _deferred_ln_step(compute_y, y_ref, g_ref, b_ref, o_ref):
    i = pl.program_id(0)
    last = pl.num_programs(0) - 1

    @pl.when(i == 0)
    def _():
        y_ref[...] = jnp.zeros_like(y_ref)

    def normalise_previous():
        o_ref[...] = _layer_norm(y_ref[...], g_ref[...], b_ref[...])

    @pl.when(i < last)
    def _():
        normalise_previous()
        y_ref[...] = compute_y()

    @pl.when(i == last)
    def _():
        normalise_previous()


def _row_specs(n_rows, tm):
    n_blk = n_rows // tm
    in_map = lambda i: (jnp.minimum(i, n_blk - 1), 0)
    out_map = lambda i: (jnp.maximum(i - 1, 0), 0)
    return (n_blk + 1,), in_map, out_map


_CONST2 = lambda i: (0, 0)
_MLP_PARAMS = dict(
    compiler_params=pltpu.CompilerParams(
        dimension_semantics=("arbitrary",), vmem_limit_bytes=VMEM_LIMIT))


def _outproj0_kernel(attn_ref, conv_ref, x_ref, w_ref, g_ref, b_ref, o_ref, y_ref):
    def compute_y():
        m = (_dot(attn_ref[...], w_ref[0:FOX_WIDTH, :])
             + _dot(conv_ref[...], w_ref[FOX_WIDTH:FOX_WIDTH + CONV_WIDTH, :]))
        return ALPHA * x_ref[...] + m

    _deferred_ln_step(compute_y, y_ref, g_ref, b_ref, o_ref)


def _outproj0(attn, conv, x2, w_out, g, b):
    n = x2.shape[0]
    tm = ROWS_MLP
    grid, row_in, row_out = _row_specs(n, tm)
    return pl.pallas_call(
        _outproj0_kernel,
        grid=grid,
        in_specs=[
            pl.BlockSpec((tm, FOX_WIDTH), row_in),
            pl.BlockSpec((tm, CONV_WIDTH), row_in),
            pl.BlockSpec((tm, D_MODEL), row_in),
            pl.BlockSpec((FOX_WIDTH + CONV_WIDTH, D_MODEL), _CONST2),
            pl.BlockSpec((1, D_MODEL), _CONST2),
            pl.BlockSpec((1, D_MODEL), _CONST2),
        ],
        out_specs=pl.BlockSpec((tm, D_MODEL), row_out),
        out_shape=jax.ShapeDtypeStruct((n, D_MODEL), F32),
        scratch_shapes=[pltpu.VMEM((tm, D_MODEL), F32)],
        name="outproj0_ln",
        **_MLP_PARAMS,
    )(attn, conv, x2, w_out, g, b)


def _ffn_kernel(x_ref, win_ref, wout_ref, g_ref, b_ref, o_ref, y_ref, xb_ref, a_ref):
    def compute_y():
        xb_ref[...] = x_ref[...].astype(BF16)
        xb = xb_ref[...]
        for c in range(FFN_HIDDEN // FFN_CHUNK):
            lo = c * FFN_CHUNK
            gate = _dot(xb, win_ref[:, lo:lo + FFN_CHUNK])
            up = _dot(xb, win_ref[:, FFN_HIDDEN + lo:FFN_HIDDEN + lo + FFN_CHUNK])
            a_ref[:, lo:lo + FFN_CHUNK] = (
                gate * (1.0 / (1.0 + jnp.exp(-gate))) * up).astype(BF16)
        return ALPHA * x_ref[...] + _dot(a_ref[...], wout_ref[...])

    _deferred_ln_step(compute_y, y_ref, g_ref, b_ref, o_ref)


def _ffn(x2, w_in, w_out, g, b):
    n = x2.shape[0]
    tm = ROWS_MLP
    grid, row_in, row_out = _row_specs(n, tm)
    return pl.pallas_call(
        _ffn_kernel,
        grid=grid,
        in_specs=[
            pl.BlockSpec((tm, D_MODEL), row_in),
            pl.BlockSpec((D_MODEL, 2 * FFN_HIDDEN), _CONST2),
            pl.BlockSpec((FFN_HIDDEN, D_MODEL), _CONST2),
            pl.BlockSpec((1, D_MODEL), _CONST2),
            pl.BlockSpec((1, D_MODEL), _CONST2),
        ],
        out_specs=pl.BlockSpec((tm, D_MODEL), row_out),
        out_shape=jax.ShapeDtypeStruct((n, D_MODEL), F32),
        scratch_shapes=[
            pltpu.VMEM((tm, D_MODEL), F32),
            pltpu.VMEM((tm, D_MODEL), BF16),
            pltpu.VMEM((tm, FFN_HIDDEN), BF16),
        ],
        name="swiglu_ln",
        **_MLP_PARAMS,
    )(x2, w_in, w_out, g, b)


def _gmlp_kernel(x_ref, win_ref, vg_ref, vb_ref, ws_ref, bst_ref, wout_ref, g_ref, b_ref,
                 o_ref, y_ref, xb_ref, u_ref, v_ref, vn_ref, t_ref):
    tm = x_ref.shape[0]
    n_blk = tm // GMLP_BLOCK

    def compute_y():
        xb_ref[...] = x_ref[...].astype(BF16)
        xb = xb_ref[...]
        for c in range(2 * D_MODEL // GMLP_IN_CHUNK):
            lo = c * GMLP_IN_CHUNK
            t = _dot(xb, win_ref[:, lo:lo + GMLP_IN_CHUNK])
            t = 0.5 * t * (1.0 + lax.erf(t * (2.0 ** -0.5)))
            if lo < D_MODEL:
                u_ref[:, lo:lo + GMLP_IN_CHUNK] = t
            else:
                v_ref[:, lo - D_MODEL:lo - D_MODEL + GMLP_IN_CHUNK] = t
        vn_ref[...] = _layer_norm(v_ref[...], vg_ref[...], vb_ref[...]).astype(BF16)

        pos_i = lax.broadcasted_iota(jnp.int32, (GMLP_BLOCK, GMLP_BLOCK), 0) // CHUNK
        pos_j = lax.broadcasted_iota(jnp.int32, (GMLP_BLOCK, GMLP_BLOCK), 1) // CHUNK
        causal = pos_j <= pos_i
        for g in range(GMLP_GROUPS):
            gl = slice(g * GMLP_BLOCK, (g + 1) * GMLP_BLOCK)
            w = jnp.where(causal, ws_ref[g], 0.0).astype(BF16)
            rhs = jnp.concatenate(
                [vn_ref[r * GMLP_BLOCK:(r + 1) * GMLP_BLOCK, gl] for r in range(n_blk)],
                axis=1)
            sg = _dot(w, rhs) + bst_ref[:, g:g + 1]
            for r in range(n_blk):
                rows = slice(r * GMLP_BLOCK, (r + 1) * GMLP_BLOCK)
                t_ref[rows, gl] = (u_ref[rows, gl]
                                   * sg[:, r * GMLP_BLOCK:(r + 1) * GMLP_BLOCK]).astype(BF16)
        return ALPHA * x_ref[...] + _dot(t_ref[...], wout_ref[...])

    _deferred_ln_step(compute_y, y_ref, g_ref, b_ref, o_ref)


def _gmlp(x2, w_in, v_g, v_b, w_s, b_s_t, w_out, g, b):
    n = x2.shape[0]
    tm = ROWS_MLP
    grid, row_in, row_out = _row_specs(n, tm)
    return pl.pallas_call(
        _gmlp_kernel,
        grid=grid,
        in_specs=[
            pl.BlockSpec((tm, D_MODEL), row_in),
            pl.BlockSpec((D_MODEL, 2 * D_MODEL), _CONST2),
            pl.BlockSpec((1, D_MODEL), _CONST2),
            pl.BlockSpec((1, D_MODEL), _CONST2),
            pl.BlockSpec((GMLP_GROUPS, GMLP_BLOCK, GMLP_BLOCK), lambda i: (0, 0, 0)),
            pl.BlockSpec((GMLP_BLOCK, GMLP_GROUPS), _CONST2),
            pl.BlockSpec((D_MODEL, D_MODEL), _CONST2),
            pl.BlockSpec((1, D_MODEL), _CONST2),
            pl.BlockSpec((1, D_MODEL), _CONST2),
        ],
        out_specs=pl.BlockSpec((tm, D_MODEL), row_out),
        out_shape=jax.ShapeDtypeStruct((n, D_MODEL), F32),
        scratch_shapes=[
            pltpu.VMEM((tm, D_MODEL), F32),
            pltpu.VMEM((tm, D_MODEL), BF16),
            pltpu.VMEM((tm, D_MODEL), F32),
            pltpu.VMEM((tm, D_MODEL), F32),
            pltpu.VMEM((tm, D_MODEL), BF16),
            pltpu.VMEM((tm, D_MODEL), BF16),
        ],
        name="gmlp_ln",
        **_MLP_PARAMS,
    )(x2, w_in, v_g, v_b, w_s, b_s_t, w_out, g, b)


def _bias_placement():
    place = np.zeros((3 * LANES, 2 * LANES), np.float32)
    ones = np.zeros((1, 2 * LANES), np.float32)
    for h in range(HEADS):
        for part in range(3):
            place[part * LANES + h, BIAS_LANES * h + part] = 1.0
            place[part * LANES + h, LANES + BIAS_LANES * h + 3 + part] = -1.0
            ones[0, BIAS_LANES * h + 3 + part] = 1.0
            ones[0, LANES + BIAS_LANES * h + part] = 1.0
    return jnp.asarray(place, BF16), jnp.asarray(ones, F32)


def kernel(x, even_w_in, even_b_f, even_conv_w, even_w_out, odd_w_in, odd_v_ln_g, odd_v_ln_b,
           odd_w_s, odd_b_s, odd_w_out, mix_ln_g, mix_ln_b, ffn_w_in, ffn_w_out, ffn_ln_g,
           ffn_ln_b):
    bsz, s_len, _ = x.shape
    n = bsz * s_len
    row = lambda a: a.reshape(1, -1)

    w0 = even_w_in[0]
    v_lo, v_hi = 2 * FOX_WIDTH, 3 * FOX_WIDTH
    w_all = jnp.concatenate(
        [w0[:, :v_lo], w0[:, v_hi + HEADS:], w0[:, v_hi:v_hi + HEADS],
         jnp.zeros((D_MODEL, F_PAD - HEADS), F32)], axis=1).astype(BF16)
    wv_t = w0[:, v_lo:v_hi].T.astype(BF16)
    bf_pad = jnp.pad(even_b_f[0], (0, F_PAD - HEADS)).reshape(1, F_PAD)
    tri = jnp.tril(jnp.ones((ROWS_IN0, ROWS_IN0), BF16))
    place, ones_row = _bias_placement()

    q, k, vt, conv, qa, ka = _inproj0(x, w_all, wv_t, bf_pad, even_conv_w[0], tri, place,
                                      ones_row)
    attn = _attention(q, qa, k, ka, vt)
    x2 = x.reshape(n, D_MODEL)
    x2 = _outproj0(attn.reshape(n, FOX_WIDTH), conv.reshape(n, CONV_WIDTH), x2,
                   even_w_out[0].astype(BF16), row(mix_ln_g[0]), row(mix_ln_b[0]))
    x2 = _ffn(x2, ffn_w_in[0].astype(BF16), ffn_w_out[0].astype(BF16),
              row(ffn_ln_g[0]), row(ffn_ln_b[0]))
    x2 = _gmlp(x2, odd_w_in[0].astype(BF16), row(odd_v_ln_g[0]), row(odd_v_ln_b[0]),
               odd_w_s[0], odd_b_s[0].T, odd_w_out[0].astype(BF16),
               row(mix_ln_g[1]), row(mix_ln_b[1]))
    x2 = _ffn(x2, ffn_w_in[1].astype(BF16), ffn_w_out[1].astype(BF16),
              row(ffn_ln_g[1]), row(ffn_ln_b[1]))
    return x2.reshape(bsz, s_len, D_MODEL)
```

```python
import jax
import jax.numpy as jnp
import numpy as np
from jax import lax
from jax.experimental import pallas as pl
from jax.experimental.pallas import tpu as pltpu

D_MODEL = 1024
HEADS = 8
HEAD_DIM = 64
FOX_WIDTH = HEADS * HEAD_DIM
CONV_WIDTH = 512
CONV_K = 3
GMLP_BLOCK = 128
GMLP_GROUPS = 8
CHUNK = 64
FFN_HIDDEN = 2816
DEPTH = 2
ALPHA = (2.0 * DEPTH) ** 0.25
LN_EPS = 1e-5

LANES = 128
F_PAD = LANES
IN0_COLS = 2 * FOX_WIDTH + 3 * CONV_WIDTH + F_PAD
BIAS_LANES = 6
LOG2E = 1.4426950408889634
Q_SCALE = HEAD_DIM ** -0.5 * LOG2E
VMEM_LIMIT = 56 * 1024 * 1024

ROWS_IN0 = 512
ROWS_MLP = 512
ATT_TQ = 512
ATT_TK = 512
ATT_UNIT = 256
ATT_PAIRS = 2
FFN_CHUNK = 256
GMLP_IN_CHUNK = 512

BF16 = jnp.bfloat16
F32 = jnp.float32


def _dot(a, b):
    return jnp.dot(a, b, preferred_element_type=F32)


def _layer_norm(y, g, b):
    mu = jnp.mean(y, axis=-1, keepdims=True)
    yc = y - mu
    var = jnp.mean(yc * yc, axis=-1, keepdims=True)
    return yc * lax.rsqrt(var + LN_EPS) * g + b


def _split3(a):
    hi = a.astype(BF16)
    r1 = a - hi.astype(F32)
    mid = r1.astype(BF16)
    lo = (r1 - mid.astype(F32)).astype(BF16)
    return hi, mid, lo


def _inproj0_kernel(x_ref, w_ref, wvt_ref, bf_ref, cw_ref, tri_ref, place_ref, ones_ref,
                    q_ref, k_ref, vt_ref, conv_ref, qa_ref, ka_ref,
                    xb_ref, zbuf_ref, carry_ref):
    tm = x_ref.shape[0]

    @pl.when(pl.program_id(1) == 0)
    def _():
        zbuf_ref[0:8, :] = jnp.zeros((8, CONV_WIDTH), F32)
        carry_ref[...] = jnp.zeros_like(carry_ref)

    xb_ref[...] = x_ref[...].astype(BF16)
    xb = xb_ref[...]
    w = FOX_WIDTH
    o = 2 * w

    fl = _dot(xb, w_ref[:, o + 3 * CONV_WIDTH:o + 3 * CONV_WIDTH + F_PAD]) + bf_ref[...]
    logf = jnp.minimum(fl, 0.0) - jnp.log1p(jnp.exp(-jnp.abs(fl)))
    lane = lax.broadcasted_iota(jnp.int32, logf.shape, 1)
    logf = jnp.where(lane < HEADS, logf, 0.0)
    parts = jnp.concatenate(_split3(logf), axis=1)
    sums = _dot(tri_ref[...], parts)
    c = ((sums[:, 0:LANES] + sums[:, LANES:2 * LANES]) + sums[:, 2 * LANES:3 * LANES]
         + carry_ref[...])
    carry_ref[...] = c[tm - 1:tm, :]

    parts = jnp.concatenate(_split3(c * LOG2E), axis=1)
    aug = _dot(parts, place_ref[...]) + ones_ref[...]
    qa_ref[...] = aug[:, 0:LANES].astype(BF16)
    ka_ref[...] = aug[:, LANES:2 * LANES].astype(BF16)

    q_ref[...] = (_dot(xb, w_ref[:, 0:w]) * Q_SCALE).astype(BF16)
    k_ref[...] = _dot(xb, w_ref[:, w:2 * w]).astype(BF16)
    vt = lax.dot_general(wvt_ref[...], xb, (((1,), (1,)), ((), ())),
                         preferred_element_type=F32)
    vt_ref[...] = vt.astype(BF16).reshape(HEADS // 2, LANES, tm)

    cg = _dot(xb, w_ref[:, o + CONV_WIDTH:o + 2 * CONV_WIDTH])
    hh = _dot(xb, w_ref[:, o + 2 * CONV_WIDTH:o + 3 * CONV_WIDTH])
    z = cg * hh
    zbuf_ref[8:8 + tm, :] = z
    y = (cw_ref[0:1, :] * zbuf_ref[6:6 + tm, :]
         + cw_ref[1:2, :] * zbuf_ref[7:7 + tm, :]
         + cw_ref[2:3, :] * z)
    bg = _dot(xb, w_ref[:, o:o + CONV_WIDTH])
    conv_ref[...] = (bg * y).astype(BF16)
    zbuf_ref[0:8, :] = zbuf_ref[tm:tm + 8, :]


def _inproj0(x, w_all, wv_t, bf_pad, conv_w, tri, place, ones_row):
    bsz, s_len, _ = x.shape
    tm = ROWS_IN0
    grid = (bsz, s_len // tm)
    row = lambda b, i: (b, i, 0)
    const2 = lambda b, i: (0, 0)
    out_bf = jax.ShapeDtypeStruct((bsz, s_len, FOX_WIDTH), BF16)
    out_aug = jax.ShapeDtypeStruct((bsz, s_len, LANES), BF16)
    return pl.pallas_call(
        _inproj0_kernel,
        grid=grid,
        in_specs=[
            pl.BlockSpec((None, tm, D_MODEL), row),
            pl.BlockSpec((D_MODEL, IN0_COLS), const2),
            pl.BlockSpec((FOX_WIDTH, D_MODEL), const2),
            pl.BlockSpec((1, F_PAD), const2),
            pl.BlockSpec((CONV_K, CONV_WIDTH), const2),
            pl.BlockSpec((tm, tm), const2),
            pl.BlockSpec((3 * LANES, 2 * LANES), const2),
            pl.BlockSpec((1, 2 * LANES), const2),
        ],
        out_specs=[
            pl.BlockSpec((None, tm, FOX_WIDTH), row),
            pl.BlockSpec((None, tm, FOX_WIDTH), row),
            pl.BlockSpec((None, HEADS // 2, None, LANES, tm), lambda b, i: (b, 0, i, 0, 0)),
            pl.BlockSpec((None, tm, CONV_WIDTH), row),
            pl.BlockSpec((None, tm, LANES), row),
            pl.BlockSpec((None, tm, LANES), row),
        ],
        out_shape=[
            out_bf, out_bf,
            jax.ShapeDtypeStruct((bsz, HEADS // 2, s_len // tm, LANES, tm), BF16),
            jax.ShapeDtypeStruct((bsz, s_len, CONV_WIDTH), BF16),
            out_aug, out_aug,
        ],
        scratch_shapes=[
            pltpu.VMEM((tm, D_MODEL), BF16),
            pltpu.VMEM((tm + 8, CONV_WIDTH), F32),
            pltpu.VMEM((1, LANES), F32),
        ],
        compiler_params=pltpu.CompilerParams(
            dimension_semantics=("arbitrary", "arbitrary"),
            vmem_limit_bytes=VMEM_LIMIT),
        name="inproj0",
    )(x, w_all, wv_t, bf_pad, conv_w, tri, place, ones_row)


def _attn_kernel(q_ref, qa_ref, k_ref, ka_ref, vt_ref, o_ref,
                 qt_ref, s_ref, mcur_ref, m_ref, l_ref, acc_ref):
    tq = q_ref.shape[0]
    tk = ATT_TK
    group = pl.program_id(1)
    qi = pl.program_id(2)
    diag = (qi * tq) // tk

    lane = lax.broadcasted_iota(jnp.int32, (tq, LANES), 1)
    qa = qa_ref[...]
    for pr in range(ATT_PAIRS):
        q2 = q_ref[:, pr * LANES:(pr + 1) * LANES]
        for hh in range(2):
            head = 2 * (ATT_PAIRS * group + pr) + hh
            own_q = (lane >= hh * HEAD_DIM) & (lane < (hh + 1) * HEAD_DIM)
            own_a = (lane >= BIAS_LANES * head) & (lane < BIAS_LANES * (head + 1))
            qfull = jnp.concatenate(
                [jnp.where(own_q, q2, jnp.zeros_like(q2)),
                 jnp.where(own_a, qa, jnp.zeros_like(qa))], axis=1)
            qt_ref[2 * pr + hh] = qfull.astype(F32).T.astype(BF16)

    m_ref[...] = jnp.full(m_ref.shape, -jnp.inf, F32)
    l_ref[...] = jnp.zeros(l_ref.shape, F32)
    acc_ref[...] = jnp.zeros(acc_ref.shape, F32)

    n_parts = tq // ATT_UNIT
    units = [(pr, hh, qh) for pr in range(ATT_PAIRS) for hh in range(2) for qh in range(n_parts)]

    def scores(j, slot, masked, u):
        pr, hh, qh = units[u]
        ks = pl.multiple_of(j * tk, tk)
        lhs = jnp.concatenate([k_ref[pl.ds(ks, tk), pr * LANES:(pr + 1) * LANES],
                               ka_ref[pl.ds(ks, tk), :]], axis=1)
        st = _dot(lhs, qt_ref[2 * pr + hh, :, qh * ATT_UNIT:(qh + 1) * ATT_UNIT])
        if masked:
            key = lax.broadcasted_iota(jnp.int32, (tk, ATT_UNIT), 0)
            qry = lax.broadcasted_iota(jnp.int32, (tk, ATT_UNIT), 1) + qh * ATT_UNIT
            st = jnp.where(key <= qry, st, -jnp.inf)
        s_ref[slot, u] = st
        mcur_ref[slot, u] = jnp.max(st.reshape(tk // 8, 8, ATT_UNIT), axis=0)

    def softmax_pv(j, slot, u):
        pr, hh, _ = units[u]
        s3 = s_ref[slot, u].reshape(tk // 8, 8, ATT_UNIT)
        m_prev = m_ref[u]
        m_new = jnp.maximum(m_prev, jnp.max(mcur_ref[slot, u], axis=0, keepdims=True))
        alpha = jnp.exp2(m_prev - m_new)
        m_ref[u] = m_new
        p = jnp.exp2((s3 - m_new[None]).reshape(tk, ATT_UNIT).astype(BF16))
        v_ones = jnp.concatenate(
            [vt_ref[pr, j, hh * HEAD_DIM:(hh + 1) * HEAD_DIM, :], jnp.ones((16, tk), BF16)],
            axis=0)
        pv = _dot(v_ones, p)
        l_ref[u] = alpha * l_ref[u] + pv[HEAD_DIM:HEAD_DIM + 8, :]
        acc_ref[u] = alpha[0:1, :] * acc_ref[u] + pv[0:HEAD_DIM, :]

    def tile_at(n):
        return jnp.where(n == 0, diag, n - 1)

    def step(score_tile, score_slot, pv_tile, pv_slot):
        for u in range(len(units)):
            if score_tile is not None:
                scores(score_tile, score_slot, False, u)
            softmax_pv(pv_tile, pv_slot, u)

    for u in range(len(units)):
        scores(diag, 0, True, u)

    def body(i, carry):
        n = 2 * i
        step(n, 1, tile_at(n), 0)
        step(n + 1, 0, n, 1)
        return carry

    lax.fori_loop(0, diag // 2, body, 0)

    @pl.when(diag % 2 == 1)
    def _():
        step(diag - 1, 1, tile_at(diag - 1), 0)
        step(None, None, diag - 1, 1)

    @pl.when(diag % 2 == 0)
    def _():
        step(None, None, tile_at(diag), 0)

    for pr in range(ATT_PAIRS):
        for qh in range(n_parts):
            halves = []
            for hh in range(2):
                u = units.index((pr, hh, qh))
                halves.append(acc_ref[u] / l_ref[u][0:1, :])
            o_ref[qh * ATT_UNIT:(qh + 1) * ATT_UNIT, pr * LANES:(pr + 1) * LANES] = (
                jnp.concatenate(halves, axis=0).T.astype(o_ref.dtype))


def _attention(q, qa, k, ka, vt):
    bsz, s_len, _ = q.shape
    tq, tk = ATT_TQ, ATT_TK
    assert tq == tk == ROWS_IN0
    n_heads_step = 2 * ATT_PAIRS
    n_units = n_heads_step * (tq // ATT_UNIT)
    width = ATT_PAIRS * LANES
    grid = (bsz, HEADS // n_heads_step, s_len // tq)
    return pl.pallas_call(
        _attn_kernel,
        grid=grid,
        in_specs=[
            pl.BlockSpec((None, tq, width), lambda b, g, i: (b, i, g)),
            pl.BlockSpec((None, tq, LANES), lambda b, g, i: (b, i, 0)),
            pl.BlockSpec((None, s_len, width), lambda b, g, i: (b, 0, g)),
            pl.BlockSpec((None, s_len, LANES), lambda b, g, i: (b, 0, 0)),
            pl.BlockSpec((None, ATT_PAIRS, s_len // tk, LANES, tk),
                         lambda b, g, i: (b, g, 0, 0, 0)),
        ],
        out_specs=pl.BlockSpec((None, tq, width), lambda b, g, i: (b, i, g)),
        out_shape=jax.ShapeDtypeStruct((bsz, s_len, FOX_WIDTH), BF16),
        scratch_shapes=[
            pltpu.VMEM((n_heads_step, 2 * LANES, tq), BF16),
            pltpu.VMEM((2, n_units, tk, ATT_UNIT), F32),
            pltpu.VMEM((2, n_units, 8, ATT_UNIT), F32),
            pltpu.VMEM((n_units, 8, ATT_UNIT), F32),
            pltpu.VMEM((n_units, 8, ATT_UNIT), F32),
            pltpu.VMEM((n_units, HEAD_DIM, ATT_UNIT), F32),
        ],
        compiler_params=pltpu.CompilerParams(
            dimension_semantics=("arbitrary", "arbitrary", "arbitrary"),
            vmem_limit_bytes=VMEM_LIMIT),
        name="fox_attention",
    )(q, qa, k, ka, vt)


def _deferred_ln_step(compute_y, y_ref, g_ref, b_ref, o_ref):
    i = pl.program_id(0)
    last = pl.num_programs(0) - 1

    @pl.when(i == 0)
    def _():
        y_ref[...] = jnp.zeros_like(y_ref)

    def normalise_previous():
        o_ref[...] = _layer_norm(y_ref[...], g_ref[...], b_ref[...])

    @pl.when(i < last)
    def _():
        normalise_previous()
        y_ref[...] = compute_y()

    @pl.when(i == last)
    def _():
        normalise_previous()


def _row_specs(n_rows, tm):
    n_blk = n_rows // tm
    in_map = lambda i: (jnp.minimum(i, n_blk - 1), 0)
    out_map = lambda i: (jnp.maximum(i - 1, 0), 0)
    return (n_blk + 1,), in_map, out_map


_CONST2 = lambda i: (0, 0)
_MLP_PARAMS = dict(
    compiler_params=pltpu.CompilerParams(
        dimension_semantics=("arbitrary",), vmem_limit_bytes=VMEM_LIMIT))


def _outproj0_kernel(attn_ref, conv_ref, x_ref, w_ref, g_ref, b_ref, o_ref, y_ref):
    def compute_y():
        m = (_dot(attn_ref[...], w_ref[0:FOX_WIDTH, :])
             + _dot(conv_ref[...], w_ref[FOX_WIDTH:FOX_WIDTH + CONV_WIDTH, :]))
        return ALPHA * x_ref[...] + m

    _deferred_ln_step(compute_y, y_ref, g_ref, b_ref, o_ref)


def _outproj0(attn, conv, x2, w_out, g, b):
    n = x2.shape[0]
    tm = ROWS_MLP
    grid, row_in, row_out = _row_specs(n, tm)
    return pl.pallas_call(
        _outproj0_kernel,
        grid=grid,
        in_specs=[
            pl.BlockSpec((tm, FOX_WIDTH), row_in),
            pl.BlockSpec((tm, CONV_WIDTH), row_in),
            pl.BlockSpec((tm, D_MODEL), row_in),
            pl.BlockSpec((FOX_WIDTH + CONV_WIDTH, D_MODEL), _CONST2),
            pl.BlockSpec((1, D_MODEL), _CONST2),
            pl.BlockSpec((1, D_MODEL), _CONST2),
        ],
        out_specs=pl.BlockSpec((tm, D_MODEL), row_out),
        out_shape=jax.ShapeDtypeStruct((n, D_MODEL), F32),
        scratch_shapes=[pltpu.VMEM((tm, D_MODEL), F32)],
        name="outproj0_ln",
        **_MLP_PARAMS,
    )(attn, conv, x2, w_out, g, b)


def _ffn_kernel(x_ref, win_ref, wout_ref, g_ref, b_ref, o_ref, y_ref, xb_ref, a_ref):
    def compute_y():
        xb_ref[...] = x_ref[...].astype(BF16)
        xb = xb_ref[...]
        for c in range(FFN_HIDDEN // FFN_CHUNK):
            lo = c * FFN_CHUNK
            gate = _dot(xb, win_ref[:, lo:lo + FFN_CHUNK])
            up = _dot(xb, win_ref[:, FFN_HIDDEN + lo:FFN_HIDDEN + lo + FFN_CHUNK])
            a_ref[:, lo:lo + FFN_CHUNK] = (
                gate * (1.0 / (1.0 + jnp.exp(-gate))) * up).astype(BF16)
        return ALPHA * x_ref[...] + _dot(a_ref[...], wout_ref[...])

    _deferred_ln_step(compute_y, y_ref, g_ref, b_ref, o_ref)


def _ffn(x2, w_in, w_out, g, b):
    n = x2.shape[0]
    tm = ROWS_MLP
    grid, row_in, row_out = _row_specs(n, tm)
    return pl.pallas_call(
        _ffn_kernel,
        grid=grid,
        in_specs=[
            pl.BlockSpec((tm, D_MODEL), row_in),
            pl.BlockSpec((D_MODEL, 2 * FFN_HIDDEN), _CONST2),
            pl.BlockSpec((FFN_HIDDEN, D_MODEL), _CONST2),
            pl.BlockSpec((1, D_MODEL), _CONST2),
            pl.BlockSpec((1, D_MODEL), _CONST2),
        ],
        out_specs=pl.BlockSpec((tm, D_MODEL), row_out),
        out_shape=jax.ShapeDtypeStruct((n, D_MODEL), F32),
        scratch_shapes=[
            pltpu.VMEM((tm, D_MODEL), F32),
            pltpu.VMEM((tm, D_MODEL), BF16),
            pltpu.VMEM((tm, FFN_HIDDEN), BF16),
        ],
        name="swiglu_ln",
        **_MLP_PARAMS,
    )(x2, w_in, w_out, g, b)


def _gmlp_kernel(x_ref, win_ref, vg_ref, vb_ref, ws_ref, bst_ref, wout_ref, g_ref, b_ref,
                 o_ref, y_ref, xb_ref, u_ref, v_ref, vn_ref, t_ref):
    tm = x_ref.shape[0]
    n_blk = tm // GMLP_BLOCK

    def compute_y():
        xb_ref[...] = x_ref[...].astype(BF16)
        xb = xb_ref[...]
        def in_proj(lo, dst_ref):
            for c in range(D_MODEL // GMLP_IN_CHUNK):
                cols = slice(c * GMLP_IN_CHUNK, (c + 1) * GMLP_IN_CHUNK)
                t = _dot(xb, win_ref[:, lo + cols.start:lo + cols.stop])
                dst_ref[:, cols] = 0.5 * t * (1.0 + lax.erf(t * (2.0 ** -0.5)))

        in_proj(D_MODEL, v_ref)
        vn_ref[...] = _layer_norm(v_ref[...], vg_ref[...], vb_ref[...]).astype(BF16)
        in_proj(0, u_ref)

        pos_i = lax.broadcasted_iota(jnp.int32, (GMLP_BLOCK, GMLP_BLOCK), 0) // CHUNK
        pos_j = lax.broadcasted_iota(jnp.int32, (GMLP_BLOCK, GMLP_BLOCK), 1) // CHUNK
        causal = pos_j <= pos_i
        for g in range(GMLP_GROUPS):
            gl = slice(g * GMLP_BLOCK, (g + 1) * GMLP_BLOCK)
            w = jnp.where(causal, ws_ref[g], 0.0).astype(BF16)
            rhs = jnp.concatenate(
                [vn_ref[r * GMLP_BLOCK:(r + 1) * GMLP_BLOCK, gl] for r in range(n_blk)],
                axis=1)
            sg = _dot(w, rhs) + bst_ref[:, g:g + 1]
            for r in range(n_blk):
                rows = slice(r * GMLP_BLOCK, (r + 1) * GMLP_BLOCK)
                t_ref[rows, gl] = (u_ref[rows, gl]
                                   * sg[:, r * GMLP_BLOCK:(r + 1) * GMLP_BLOCK]).astype(BF16)
        return ALPHA * x_ref[...] + _dot(t_ref[...], wout_ref[...])

    _deferred_ln_step(compute_y, y_ref, g_ref, b_ref, o_ref)


def _gmlp(x2, w_in, v_g, v_b, w_s, b_s_t, w_out, g, b):
    n = x2.shape[0]
    tm = ROWS_MLP
    grid, row_in, row_out = _row_specs(n, tm)
    return pl.pallas_call(
        _gmlp_kernel,
        grid=grid,
        in_specs=[
            pl.BlockSpec((tm, D_MODEL), row_in),
            pl.BlockSpec((D_MODEL, 2 * D_MODEL), _CONST2),
            pl.BlockSpec((1, D_MODEL), _CONST2),
            pl.BlockSpec((1, D_MODEL), _CONST2),
            pl.BlockSpec((GMLP_GROUPS, GMLP_BLOCK, GMLP_BLOCK), lambda i: (0, 0, 0)),
            pl.BlockSpec((GMLP_BLOCK, GMLP_GROUPS), _CONST2),
            pl.BlockSpec((D_MODEL, D_MODEL), _CONST2),
            pl.BlockSpec((1, D_MODEL), _CONST2),
            pl.BlockSpec((1, D_MODEL), _CONST2),
        ],
        out_specs=pl.BlockSpec((tm, D_MODEL), row_out),
        out_shape=jax.ShapeDtypeStruct((n, D_MODEL), F32),
        scratch_shapes=[
            pltpu.VMEM((tm, D_MODEL), F32),
            pltpu.VMEM((tm, D_MODEL), BF16),
            pltpu.VMEM((tm, D_MODEL), F32),
            pltpu.VMEM((tm, D_MODEL), F32),
            pltpu.VMEM((tm, D_MODEL), BF16),
            pltpu.VMEM((tm, D_MODEL), BF16),
        ],
        name="gmlp_ln",
        **_MLP_PARAMS,
    )(x2, w_in, v_g, v_b, w_s, b_s_t, w_out, g, b)


def _bias_placement():
    place = np.zeros((3 * LANES, 2 * LANES), np.float32)
    ones = np.zeros((1, 2 * LANES), np.float32)
    for h in range(HEADS):
        for part in range(3):
            place[part * LANES + h, BIAS_LANES * h + part] = 1.0
            place[part * LANES + h, LANES + BIAS_LANES * h + 3 + part] = -1.0
            ones[0, BIAS_LANES * h + 3 + part] = 1.0
            ones[0, LANES + BIAS_LANES * h + part] = 1.0
    return jnp.asarray(place, BF16), jnp.asarray(ones, F32)


def kernel(x, even_w_in, even_b_f, even_conv_w, even_w_out, odd_w_in, odd_v_ln_g, odd_v_ln_b,
           odd_w_s, odd_b_s, odd_w_out, mix_ln_g, mix_ln_b, ffn_w_in, ffn_w_out, ffn_ln_g,
           ffn_ln_b):
    bsz, s_len, _ = x.shape
    n = bsz * s_len
    row = lambda a: a.reshape(1, -1)

    w0 = even_w_in[0]
    v_lo, v_hi = 2 * FOX_WIDTH, 3 * FOX_WIDTH
    w_all = jnp.concatenate(
        [w0[:, :v_lo], w0[:, v_hi + HEADS:], w0[:, v_hi:v_hi + HEADS],
         jnp.zeros((D_MODEL, F_PAD - HEADS), F32)], axis=1).astype(BF16)
    wv_t = w0[:, v_lo:v_hi].T.astype(BF16)
    bf_pad = jnp.pad(even_b_f[0], (0, F_PAD - HEADS)).reshape(1, F_PAD)
    tri = jnp.tril(jnp.ones((ROWS_IN0, ROWS_IN0), BF16))
    place, ones_row = _bias_placement()

    q, k, vt, conv, qa, ka = _inproj0(x, w_all, wv_t, bf_pad, even_conv_w[0], tri, place,
                                      ones_row)
    attn = _attention(q, qa, k, ka, vt)
    x2 = x.reshape(n, D_MODEL)
    x2 = _outproj0(attn.reshape(n, FOX_WIDTH), conv.reshape(n, CONV_WIDTH), x2,
                   even_w_out[0].astype(BF16), row(mix_ln_g[0]), row(mix_ln_b[0]))
    x2 = _ffn(x2, ffn_w_in[0].astype(BF16), ffn_w_out[0].astype(BF16),
              row(ffn_ln_g[0]), row(ffn_ln_b[0]))
    x2 = _gmlp(x2, odd_w_in[0].astype(BF16), row(odd_v_ln_g[0]), row(odd_v_ln_b[0]),
               odd_w_s[0], odd_b_s[0].T, odd_w_out[0].astype(BF16),
               row(mix_ln_g[1]), row(mix_ln_b[1]))
    x2 = _ffn(x2, ffn_w_in[1].astype(BF16), ffn_w_out[1].astype(BF16),
              row(ffn_ln_g[1]), row(ffn_ln_b[1]))
    return x2.reshape(bsz, s_len, D_MODEL)
```
